```python
import math
import jax, jax.numpy as jnp
from jax import lax
import numpy as np

D_MODEL = 1024
BATCH = 4
SEQ = 8192
DEPTH = 2

D_MIX = D_MODEL
D_FF = 2816
NORM_EPS = 1e-6

POOL_WINDOWS = (2, 4, 8, 16)
POOL_GROUPS = len(POOL_WINDOWS)
POOL_GROUP_DIM = 64
POOL_WIDTH = POOL_GROUPS * POOL_GROUP_DIM

ATTN_HEADS = 8
HEAD_DIM = 64
ATTN_WIDTH = ATTN_HEADS * HEAD_DIM
Q_BLOCK = 128

CONV_WIDTH_CH = D_MIX - POOL_WIDTH - ATTN_WIDTH
CONV_KERNEL = 31

IN_COLS = POOL_WIDTH + 3 * ATTN_WIDTH + ATTN_HEADS + 2 * CONV_WIDTH_CH

kernel_name = "hymba_style_pool_fox_conformer_macaron"


def rms_norm(x, g):
    x32 = x.astype(jnp.float32)
    y = x32 * lax.rsqrt(jnp.mean(x32 * x32, axis=-1, keepdims=True) + NORM_EPS)
    return (y * g.astype(jnp.float32)).astype(x.dtype)


def layer_norm(x, g, b):
    x32 = x.astype(jnp.float32)
    mu = jnp.mean(x32, axis=-1, keepdims=True)
    xc = x32 - mu
    var = jnp.mean(xc * xc, axis=-1, keepdims=True)
    y = xc * lax.rsqrt(var + NORM_EPS)
    return (y * g.astype(jnp.float32) + b.astype(jnp.float32)).astype(x.dtype)


def swiglu(h, w_gate, w_up, w_down):
    return (jax.nn.silu(h @ w_gate) * (h @ w_up)) @ w_down


def causal_window_mean(u, w):
    S = u.shape[1]
    u32 = u.astype(jnp.float32)
    cs = jnp.cumsum(u32, axis=1)
    lagged = jnp.pad(cs, ((0, 0), (w, 0), (0, 0)))[:, :S]
    count = jnp.minimum(jnp.arange(S) + 1, w).astype(jnp.float32)
    return ((cs - lagged) / count[None, :, None]).astype(u.dtype)


def pool_mixer(u, pool_w, pool_scale):
    B, S, _ = u.shape
    ug = u.reshape(B, S, POOL_GROUPS, POOL_GROUP_DIM)
    pooled = jnp.stack(
        [causal_window_mean(ug[:, :, g], w) - ug[:, :, g] for g, w in enumerate(POOL_WINDOWS)],
        axis=2)
    mixed = jnp.einsum('bsgc,gcd->bsgd', pooled, pool_w)
    return mixed.reshape(B, S, POOL_WIDTH) * pool_scale


def forgetting_attention(q, k, v, z_f, forget_bias):
    B, S, H, Dh = q.shape
    n_blk = S // Q_BLOCK
    scale = 1.0 / math.sqrt(Dh)
    log_f = jax.nn.log_sigmoid(z_f.astype(jnp.float32) + forget_bias.astype(jnp.float32))
    F = jnp.cumsum(log_f, axis=1).transpose(0, 2, 1)
    qh = q.transpose(0, 2, 1, 3)
    kh = k.transpose(0, 2, 1, 3)
    vh = v.transpose(0, 2, 1, 3)
    q_blocks = qh.reshape(B, H, n_blk, Q_BLOCK, Dh).transpose(2, 0, 1, 3, 4)
    F_blocks = F.reshape(B, H, n_blk, Q_BLOCK).transpose(2, 0, 1, 3)
    k_pos = jnp.arange(S)

    def one_block(args):
        q_i, F_i, i = args
        s = jnp.einsum('bhqd,bhkd->bhqk', q_i, kh).astype(jnp.float32) * scale
        s = s + F_i[..., None] - F[:, :, None, :]
        q_pos = i * Q_BLOCK + jnp.arange(Q_BLOCK)
        mask = k_pos[None, :] <= q_pos[:, None]
        s = jnp.where(mask[None, None], s, -jnp.inf)
        p = jax.nn.softmax(s, axis=-1)
        return jnp.einsum('bhqk,bhkd->bhqd', p.astype(vh.dtype), vh)

    out = lax.map(one_block, (q_blocks, F_blocks, jnp.arange(n_blk)))
    return out.transpose(1, 0, 3, 2, 4).reshape(B, S, H * Dh)


def conformer_conv(h_glu, conv_w, conv_b, ln_g, ln_b):
    a, g = jnp.split(h_glu, 2, axis=-1)
    u = a * jax.nn.sigmoid(g)
    y = lax.conv_general_dilated(
        u, conv_w[:, None, :].astype(u.dtype), window_strides=(1,),
        padding=[(CONV_KERNEL - 1, 0)], dimension_numbers=('NWC', 'WIO', 'NWC'),
        feature_group_count=CONV_WIDTH_CH) + conv_b
    return jax.nn.silu(layer_norm(y, ln_g, ln_b))


def token_mixer(h, w_in, pool_w, pool_scale, forget_bias, conv_w, conv_b, conv_ln_g, conv_ln_b, w_out):
    B, S, _ = h.shape
    p = h @ w_in
    o = 0
    u_pool = p[..., o:o + POOL_WIDTH]; o += POOL_WIDTH
    q = p[..., o:o + ATTN_WIDTH]; o += ATTN_WIDTH
    k = p[..., o:o + ATTN_WIDTH]; o += ATTN_WIDTH
    v = p[..., o:o + ATTN_WIDTH]; o += ATTN_WIDTH
    z_f = p[..., o:o + ATTN_HEADS]; o += ATTN_HEADS
    h_glu = p[..., o:o + 2 * CONV_WIDTH_CH]
    shp = (B, S, ATTN_HEADS, HEAD_DIM)
    y_a = pool_mixer(u_pool, pool_w, pool_scale)
    y_b = forgetting_attention(q.reshape(shp), k.reshape(shp), v.reshape(shp), z_f, forget_bias)
    y_c = conformer_conv(h_glu, conv_w, conv_b, conv_ln_g, conv_ln_b)
    return jnp.concatenate([y_a, y_b, y_c], axis=-1) @ w_out


def setup_inputs(seed: int = 0) -> dict:
    key = jax.random.key(seed)
    ks = jax.random.split(key, 24)

    def nrm(k, shape, scale):
        return jax.random.normal(k, shape, jnp.float32) * scale

    def gain(k, shape):
        return 1.0 + 0.02 * jax.random.normal(k, shape, jnp.float32)

    L = DEPTH
    return {
        "x": nrm(ks[0], (BATCH, SEQ, D_MODEL), 1.0),
        "ffn1_norm": gain(ks[1], (L, D_MODEL)),
        "ffn1_w_gate": nrm(ks[2], (L, D_MODEL, D_FF), D_MODEL ** -0.5),
        "ffn1_w_up": nrm(ks[3], (L, D_MODEL, D_FF), D_MODEL ** -0.5),
        "ffn1_w_down": nrm(ks[4], (L, D_FF, D_MODEL), D_FF ** -0.5),
        "mix_norm": gain(ks[5], (L, D_MODEL)),
        "w_in": nrm(ks[6], (L, D_MODEL, IN_COLS), D_MODEL ** -0.5),
        "pool_w": nrm(ks[7], (L, POOL_GROUPS, POOL_GROUP_DIM, POOL_GROUP_DIM), POOL_GROUP_DIM ** -0.5),
        "pool_scale": gain(ks[8], (L, POOL_WIDTH)),
        "forget_bias": 2.0 + 0.1 * jax.random.normal(ks[9], (L, ATTN_HEADS), jnp.float32),
        "conv_w": nrm(ks[10], (L, CONV_KERNEL, CONV_WIDTH_CH), CONV_KERNEL ** -0.5),
        "conv_b": nrm(ks[11], (L, CONV_WIDTH_CH), 0.02),
        "conv_ln_g": gain(ks[12], (L, CONV_WIDTH_CH)),
        "conv_ln_b": nrm(ks[13], (L, CONV_WIDTH_CH), 0.02),
        "w_out": nrm(ks[14], (L, D_MIX, D_MODEL), D_MIX ** -0.5),
        "ffn2_norm": gain(ks[15], (L, D_MODEL)),
        "ffn2_w_gate": nrm(ks[16], (L, D_MODEL, D_FF), D_MODEL ** -0.5),
        "ffn2_w_up": nrm(ks[17], (L, D_MODEL, D_FF), D_MODEL ** -0.5),
        "ffn2_w_down": nrm(ks[18], (L, D_FF, D_MODEL), D_FF ** -0.5),
        "final_norm": gain(ks[19], (D_MODEL,)),
    }


def reference(x, ffn1_norm, ffn1_w_gate, ffn1_w_up, ffn1_w_down, mix_norm, w_in, pool_w, pool_scale,
              forget_bias, conv_w, conv_b, conv_ln_g, conv_ln_b, w_out, ffn2_norm, ffn2_w_gate,
              ffn2_w_up, ffn2_w_down, final_norm):
    for l in range(DEPTH):
        x = x + 0.5 * swiglu(rms_norm(x, ffn1_norm[l]), ffn1_w_gate[l], ffn1_w_up[l], ffn1_w_down[l])
        x = x + token_mixer(rms_norm(x, mix_norm[l]), w_in[l], pool_w[l], pool_scale[l], forget_bias[l],
                            conv_w[l], conv_b[l], conv_ln_g[l], conv_ln_b[l], w_out[l])
        x = x + 0.5 * swiglu(rms_norm(x, ffn2_norm[l]), ffn2_w_gate[l], ffn2_w_up[l], ffn2_w_down[l])
    return rms_norm(x, final_norm)
```

```python
import functools
import math

import jax
import jax.numpy as jnp
from jax import lax
from jax.experimental import pallas as pl
from jax.experimental.pallas import tpu as pltpu

D_MODEL = 1024
D_FF = 2816
NORM_EPS = 1e-6
POOL_WINDOWS = (2, 4, 8, 16)
POOL_GROUP_DIM = 64
POOL_WIDTH = 256
ATTN_HEADS = 8
HEAD_DIM = 64
ATTN_WIDTH = 512
CONV_CH = 256
CONV_KERNEL = 31

LANES = 128
SUBLANES = 8
MXU_DIM = 256
VMEM_LIMIT_BYTES = 56 * 1024 * 1024

FFN_TM = 512
FFN_FC = 256
MIX_TM = 512
POOL_HALO = 16
CONV_HALO = 32
SCAN_HALO = MIX_TM // 2
CONV_ROWS = 64
ATT_TQ = 512
ATT_TK = 512
OUT_TM = 512
NEG_BIG = -1e30

BF16 = jnp.bfloat16
F32 = jnp.float32


def _rms_norm(x, g):
    ms = jnp.mean(x * x, axis=-1, keepdims=True)
    return x * lax.rsqrt(ms + NORM_EPS) * g


def _silu(x):
    return x * jax.nn.sigmoid(x)


def _dot(a, b):
    return jnp.dot(a, b, preferred_element_type=F32)


def _ffn_kernel(x_ref, g_ref, wg_ref, wu_ref, wd_ref, fin_ref, o_ref, *, final_norm):
    x = x_ref[...]
    hn = _rms_norm(x, g_ref[...]).astype(BF16)
    acc = jnp.zeros(x.shape, F32)
    for c in range(D_FF // FFN_FC):
        lo = c * FFN_FC
        gate = _dot(hn, wg_ref[:, lo:lo + FFN_FC])
        up = _dot(hn, wu_ref[:, lo:lo + FFN_FC])
        act = (_silu(gate) * up).astype(BF16)
        acc = acc + _dot(act, wd_ref[lo:lo + FFN_FC, :])
    y = x + 0.5 * acc
    if final_norm:
        y = _rms_norm(y, fin_ref[...])
    o_ref[...] = y


def _ffn_call(x2d, g, wg, wu, wd, fin, final_norm):
    t = x2d.shape[0]
    const = lambda i: (0, 0)
    return pl.pallas_call(
        functools.partial(_ffn_kernel, final_norm=final_norm),
        grid=(t // FFN_TM,),
        in_specs=[
            pl.BlockSpec((FFN_TM, D_MODEL), lambda i: (i, 0)),
            pl.BlockSpec((1, D_MODEL), const),
            pl.BlockSpec((D_MODEL, D_FF), const, pipeline_mode=pl.Buffered(1)),
            pl.BlockSpec((D_MODEL, D_FF), const, pipeline_mode=pl.Buffered(1)),
            pl.BlockSpec((D_FF, D_MODEL), const, pipeline_mode=pl.Buffered(1)),
            pl.BlockSpec((1, D_MODEL), const),
        ],
        out_specs=pl.BlockSpec((FFN_TM, D_MODEL), lambda i: (i, 0)),
        out_shape=jax.ShapeDtypeStruct(x2d.shape, F32),
        compiler_params=pltpu.CompilerParams(
            dimension_semantics=("arbitrary",), vmem_limit_bytes=VMEM_LIMIT_BYTES),
        name="ffn",
    )(x2d, g, wg, wu, wd, fin)


_C_POOL = 0
_C_Q = 256
_C_K = 768
_C_V = 1280
_C_GA = 1792
_C_GG = 2048
_C_END = 2304


def _mixer_in_kernel(x_ref, g_ref, wm_ref, wzf_ref, fb_ref, pw_ref, ps_ref, cw_ref, cb_ref, lg_ref, lb_ref,
                     ya_ref, q_ref, k_ref, v_ref, f_ref, ft_ref, yc_ref,
                     pbuf, cbuf, sbuf, fcarry):
    i = pl.program_id(1)
    tm = MIX_TM

    @pl.when(i == 0)
    def _():
        pbuf[0:POOL_HALO, :] = jnp.zeros((POOL_HALO, POOL_WIDTH), F32)
        cbuf[0:CONV_HALO, :] = jnp.zeros((CONV_HALO, CONV_CH), F32)
        sbuf[0:SCAN_HALO, :] = jnp.zeros((SCAN_HALO, LANES), F32)
        fcarry[...] = jnp.zeros((1, LANES), F32)

    hn = _rms_norm(x_ref[0], g_ref[...]).astype(BF16)

    q_ref[0] = (_dot(hn, wm_ref[:, _C_Q:_C_K]) * (1.0 / math.sqrt(HEAD_DIM))).astype(BF16)
    k_ref[0] = _dot(hn, wm_ref[:, _C_K:_C_V]).astype(BF16)
    v_ref[0] = _dot(hn, wm_ref[:, _C_V:_C_GA]).astype(BF16)

    u = _dot(hn, wm_ref[:, _C_POOL:_C_Q])
    pbuf[POOL_HALO:POOL_HALO + tm, :] = u
    lane = lax.broadcasted_iota(jnp.int32, (1, POOL_WIDTH), 1)
    group = lane // POOL_GROUP_DIM
    wsum = u
    sel = jnp.zeros_like(u)
    for d in range(1, max(POOL_WINDOWS)):
        wsum = wsum + pbuf[POOL_HALO - d:POOL_HALO - d + tm, :]
        if d + 1 in POOL_WINDOWS:
            sel = jnp.where(group == POOL_WINDOWS.index(d + 1), wsum, sel)
    win = jnp.zeros((1, POOL_WIDTH), F32)
    for gi, w in enumerate(POOL_WINDOWS):
        win = jnp.where(group == gi, float(w), win)
    pos = (i * tm + lax.broadcasted_iota(jnp.int32, (tm, 1), 0) + 1).astype(F32)
    count = jnp.minimum(pos, win)
    pooled = sel / count - u
    mixed = _dot(pooled.astype(BF16), pw_ref[...])
    ya_ref[0] = (mixed * ps_ref[...]).astype(BF16)
    pbuf[0:POOL_HALO, :] = pbuf[tm:tm + POOL_HALO, :]

    ga = _dot(hn, wm_ref[:, _C_GA:_C_GG])
    gg = _dot(hn, wm_ref[:, _C_GG:_C_END])
    cbuf[CONV_HALO:CONV_HALO + tm, :] = ga * jax.nn.sigmoid(gg)
    cb = cb_ref[...]
    lg = lg_ref[...]
    lb = lb_ref[...]
    for r in range(tm // CONV_ROWS):
        base = CONV_HALO + r * CONV_ROWS - (CONV_KERNEL - 1)
        acc = jnp.broadcast_to(cb, (CONV_ROWS, CONV_CH))
        for j in range(CONV_KERNEL):
            acc = acc + cw_ref[j:j + 1, :] * cbuf[base + j:base + j + CONV_ROWS, :]
        mu = jnp.mean(acc, axis=-1, keepdims=True)
        xc = acc - mu
        var = jnp.mean(xc * xc, axis=-1, keepdims=True)
        yn = xc * lax.rsqrt(var + NORM_EPS) * lg + lb
        yc_ref[0, r * CONV_ROWS:(r + 1) * CONV_ROWS, :] = _silu(yn).astype(BF16)
    cbuf[0:CONV_HALO, :] = cbuf[tm:tm + CONV_HALO, :]

    z = _dot(hn, wzf_ref[...]) + fb_ref[...]
    logf = jnp.minimum(z, 0.0) - jnp.log1p(jnp.exp(-jnp.abs(z)))
    sbuf[SCAN_HALO:SCAN_HALO + tm, :] = logf
    d = 1
    cur = logf
    while d < tm:
        cur = sbuf[SCAN_HALO:SCAN_HALO + tm, :] + sbuf[SCAN_HALO - d:SCAN_HALO - d + tm, :]
        sbuf[SCAN_HALO:SCAN_HALO + tm, :] = cur
        d *= 2
    ftile = cur + fcarry[...]
    fcarry[...] = ftile[tm - 1:tm, :]
    f_ref[0] = ftile
    ft_ref[0] = ftile.T[0:ATTN_HEADS, :]


def _mixer_in_call(x3d, g, wm, wzf, fb, pw, ps, cw, cb, lg, lb):
    b, s, _ = x3d.shape
    tm = MIX_TM
    const = lambda bi, i: (0, 0)
    row = lambda bi, i: (bi, i, 0)
    out_shape = (
        jax.ShapeDtypeStruct((b, s, POOL_WIDTH), BF16),
        jax.ShapeDtypeStruct((b, s, ATTN_WIDTH), BF16),
        jax.ShapeDtypeStruct((b, s, ATTN_WIDTH), BF16),
        jax.ShapeDtypeStruct((b, s, ATTN_WIDTH), BF16),
        jax.ShapeDtypeStruct((b, s, LANES), F32),
        jax.ShapeDtypeStruct((b, ATTN_HEADS, s), F32),
        jax.ShapeDtypeStruct((b, s, CONV_CH), BF16),
    )
    return pl.pallas_call(
        _mixer_in_kernel,
        grid=(b, s // tm),
        in_specs=[
            pl.BlockSpec((1, tm, D_MODEL), row),
            pl.BlockSpec((1, D_MODEL), const),
            pl.BlockSpec((D_MODEL, _C_END), const, pipeline_mode=pl.Buffered(1)),
            pl.BlockSpec((D_MODEL, LANES), const),
            pl.BlockSpec((1, LANES), const),
            pl.BlockSpec((POOL_WIDTH, POOL_WIDTH), const),
            pl.BlockSpec((1, POOL_WIDTH), const),
            pl.BlockSpec((CONV_HALO, CONV_CH), const),
            pl.BlockSpec((1, CONV_CH), const),
            pl.BlockSpec((1, CONV_CH), const),
            pl.BlockSpec((1, CONV_CH), const),
        ],
        out_specs=(
            pl.BlockSpec((1, tm, POOL_WIDTH), row),
            pl.BlockSpec((1, tm, ATTN_WIDTH), row),
            pl.BlockSpec((1, tm, ATTN_WIDTH), row),
            pl.BlockSpec((1, tm, ATTN_WIDTH), row),
            pl.BlockSpec((1, tm, LANES), row),
            pl.BlockSpec((1, ATTN_HEADS, tm), lambda bi, i: (bi, 0, i)),
            pl.BlockSpec((1, tm, CONV_CH), row),
        ),
        out_shape=out_shape,
        scratch_shapes=[
            pltpu.VMEM((POOL_HALO + tm, POOL_WIDTH), F32),
            pltpu.VMEM((CONV_HALO + tm, CONV_CH), F32),
            pltpu.VMEM((SCAN_HALO + tm, LANES), F32),
            pltpu.VMEM((1, LANES), F32),
        ],
        compiler_params=pltpu.CompilerParams(
            dimension_semantics=("arbitrary", "arbitrary"), vmem_limit_bytes=VMEM_LIMIT_BYTES),
        name="mixer_in",
    )(x3d, g, wm, wzf, fb, pw, ps, cw, cb, lg, lb)


def _attn_kernel(q_ref, k_ref, v_ref, f_ref, ft_ref, o_ref):
    hp = pl.program_id(1)
    i = pl.program_id(2)
    tq, tk = ATT_TQ, ATT_TK
    q2 = q_ref[0]
    fblk = f_ref[0]
    lane = lax.broadcasted_iota(jnp.int32, (1, LANES), 1)
    nt_dims = (((1,), (1,)), ((), ()))

    qs, fqs, hs = [], [], []
    for a in range(2):
        in_half = (lane // HEAD_DIM) == a
        qs.append(jnp.where(in_half, q2, jnp.zeros_like(q2)))
        h = 2 * hp + a
        hs.append(h)
        fqs.append(jnp.sum(jnp.where(lane == h, fblk, 0.0), axis=1, keepdims=True))

    def step(j, carry, masked):
        off = pl.multiple_of(j * tk, tk)
        kj = k_ref[0, pl.ds(off, tk), :]
        vj = v_ref[0, pl.ds(off, tk), :]
        new = []
        for a in range(2):
            m, l, acc = carry[a]
            s = lax.dot_general(qs[a], kj, nt_dims, preferred_element_type=F32)
            fk = ft_ref[0, pl.ds(hs[a], 1), pl.ds(off, tk)]
            s = s + (fqs[a] - fk)
            if masked:
                row = lax.broadcasted_iota(jnp.int32, (tq, tk), 0)
                col = lax.broadcasted_iota(jnp.int32, (tq, tk), 1)
                s = jnp.where(col <= row, s, NEG_BIG)
            m_new = jnp.maximum(m, jnp.max(s, axis=1, keepdims=True))
            alpha = jnp.exp(m - m_new)
            p = jnp.exp(s - m_new)
            l = alpha * l + jnp.sum(p, axis=1, keepdims=True)
            acc = alpha * acc + _dot(p.astype(BF16), vj)
            new.append((m_new, l, acc))
        return tuple(new)

    init = tuple((jnp.full((tq, 1), NEG_BIG, F32), jnp.zeros((tq, 1), F32), jnp.zeros((tq, LANES), F32))
                 for _ in range(2))
    carry = lax.fori_loop(0, i, lambda j, c: step(j, c, False), init)
    carry = step(i, carry, True)
    o0 = carry[0][2] / carry[0][1]
    o1 = carry[1][2] / carry[1][1]
    o_ref[0] = jnp.where((lane // HEAD_DIM) == 0, o0, o1).astype(BF16)


def _attn_call(q, k, v, f, ft):
    b, s, _ = q.shape
    tq = ATT_TQ
    n_pairs = ATTN_HEADS // 2
    return pl.pallas_call(
        _attn_kernel,
        grid=(b, n_pairs, s // tq),
        in_specs=[
            pl.BlockSpec((1, tq, LANES), lambda bi, hp, i: (bi, i, hp)),
            pl.BlockSpec((1, s, LANES), lambda bi, hp, i: (bi, 0, hp)),
            pl.BlockSpec((1, s, LANES), lambda bi, hp, i: (bi, 0, hp)),
            pl.BlockSpec((1, tq, LANES), lambda bi, hp, i: (bi, i, 0)),
            pl.BlockSpec((1, ATTN_HEADS, s), lambda bi, hp, i: (bi, 0, 0)),
        ],
        out_specs=pl.BlockSpec((1, tq, LANES), lambda bi, hp, i: (bi, i, hp)),
        out_shape=jax.ShapeDtypeStruct((b, s, ATTN_WIDTH), BF16),
        compiler_params=pltpu.CompilerParams(
            dimension_semantics=("arbitrary", "arbitrary", "arbitrary"), vmem_limit_bytes=VMEM_LIMIT_BYTES),
        name="attn",
    )(q, k, v, f, ft)


def _mixer_out_kernel(x_ref, ya_ref, yb_ref, yc_ref, wo_ref, o_ref):
    a_end = POOL_WIDTH
    b_end = POOL_WIDTH + ATTN_WIDTH
    y = _dot(ya_ref[...], wo_ref[0:a_end, :])
    y = y + _dot(yb_ref[...], wo_ref[a_end:b_end, :])
    y = y + _dot(yc_ref[...], wo_ref[b_end:D_MODEL, :])
    o_ref[...] = x_ref[...] + y


def _mixer_out_call(x2d, ya, yb, yc, wo):
    t = x2d.shape[0]
    tm = OUT_TM
    row = lambda i: (i, 0)
    return pl.pallas_call(
        _mixer_out_kernel,
        grid=(t // tm,),
        in_specs=[
            pl.BlockSpec((tm, D_MODEL), row),
            pl.BlockSpec((tm, POOL_WIDTH), row),
            pl.BlockSpec((tm, ATTN_WIDTH), row),
            pl.BlockSpec((tm, CONV_CH), row),
            pl.BlockSpec((D_MODEL, D_MODEL), lambda i: (0, 0)),
        ],
        out_specs=pl.BlockSpec((tm, D_MODEL), row),
        out_shape=jax.ShapeDtypeStruct(x2d.shape, F32),
        compiler_params=pltpu.CompilerParams(
            dimension_semantics=("arbitrary",), vmem_limit_bytes=VMEM_LIMIT_BYTES),
        name="mixer_out",
    )(x2d, ya, yb, yc, wo)


def _block_diag(pool_w):
    g, c, d = pool_w.shape
    out = jnp.zeros((g * c, g * d), pool_w.dtype)
    for gi in range(g):
        out = lax.dynamic_update_slice(out, pool_w[gi], (gi * c, gi * d))
    return out


def _pad_lanes(a, width):
    return jnp.pad(a, ((0, 0), (0, width - a.shape[-1])))


def kernel(x, ffn1_norm, ffn1_w_gate, ffn1_w_up, ffn1_w_down, mix_norm, w_in, pool_w, pool_scale, forget_bias, conv_w, conv_b, conv_ln_g, conv_ln_b, w_out, ffn2_norm, ffn2_w_gate, ffn2_w_up, ffn2_w_down, final_norm):
    b, s, d = x.shape
    depth = w_in.shape[0]
    t = b * s
    x2d = x.reshape(t, d)
    fin = final_norm.reshape(1, d)
    zf_lo = POOL_WIDTH + 3 * ATTN_WIDTH
    glu_lo = zf_lo + ATTN_HEADS
    for l in range(depth):
        x2d = _ffn_call(x2d, ffn1_norm[l].reshape(1, d), ffn1_w_gate[l].astype(BF16), ffn1_w_up[l].astype(BF16),
                        ffn1_w_down[l].astype(BF16), fin, False)
        wl = w_in[l]
        wm = jnp.concatenate([wl[:, :zf_lo], wl[:, glu_lo:]], axis=1).astype(BF16)
        wzf = _pad_lanes(wl[:, zf_lo:glu_lo], LANES).astype(BF16)
        fb = _pad_lanes(forget_bias[l].reshape(1, ATTN_HEADS), LANES)
        pw = _block_diag(pool_w[l]).astype(BF16)
        cw = jnp.pad(conv_w[l], ((0, CONV_HALO - CONV_KERNEL), (0, 0)))
        ya, q, k, v, f, ft, yc = _mixer_in_call(
            x2d.reshape(b, s, d), mix_norm[l].reshape(1, d), wm, wzf, fb, pw,
            pool_scale[l].reshape(1, POOL_WIDTH), cw, conv_b[l].reshape(1, CONV_CH),
            conv_ln_g[l].reshape(1, CONV_CH), conv_ln_b[l].reshape(1, CONV_CH))
        yb = _attn_call(q, k, v, f, ft)
        x2d = _mixer_out_call(x2d, ya.reshape(t, POOL_WIDTH), yb.reshape(t, ATTN_WIDTH),
                              yc.reshape(t, CONV_CH), w_out[l].astype(BF16))
        x2d = _ffn_call(x2d, ffn2_norm[l].reshape(1, d), ffn2_w_gate[l].astype(BF16), ffn2_w_up[l].astype(BF16),
                        ffn2_w_down[l].astype(BF16), fin, l == depth - 1)
    return x2d.reshape(b, s, d)
```

```python
import functools
import math

import numpy as np
import jax
import jax.numpy as jnp
from jax import lax
from jax.experimental import pallas as pl
from jax.experimental.pallas import tpu as pltpu

D_MODEL = 1024
D_FF = 2816
NORM_EPS = 1e-6
POOL_WINDOWS = (2, 4, 8, 16)
POOL_GROUP_DIM = 64
POOL_WIDTH = 256
ATTN_HEADS = 8
HEAD_DIM = 64
ATTN_WIDTH = 512
CONV_CH = 256
CONV_KERNEL = 31

LANES = 128
SUBLANES = 8
VMEM_LIMIT_BYTES = 56 * 1024 * 1024

FFN_TM = 512
FFN_FC = 256
MIX_TM = 512
POOL_HALO = 16
CONV_HALO = 32
SCAN_HALO = MIX_TM // 2
CONV_ROWS = 64
SLAB = LANES
ATT_TQ = 512
ATT_QS = 256
ATT_CK = 128
ATT_KB = 512
ATT_HG = 2
OUT_TM = 512
NEG_BIG = -1e30
LOG2E = math.log2(math.e)
ATT_VROWS = HEAD_DIM + 16

FS_ONE = 3 * ATTN_HEADS

BF16 = jnp.bfloat16
F32 = jnp.float32
NT_DIMS = (((1,), (1,)), ((), ()))
TN_DIMS = (((0,), (0,)), ((), ()))


def _rms_norm(x, g):
    ms = jnp.mean(x * x, axis=-1, keepdims=True)
    return x * lax.rsqrt(ms + NORM_EPS) * g


def _silu(x):
    return x * jax.nn.sigmoid(x)


def _dot(a, b):
    return jnp.dot(a, b, preferred_element_type=F32)


def _ffn_kernel(x_ref, g_ref, wg_ref, wu_ref, wd_ref, fin_ref, o_ref, *, final_norm):
    x = x_ref[...]
    hn = _rms_norm(x, g_ref[...]).astype(BF16)
    acc = jnp.zeros(x.shape, F32)
    for c in range(D_FF // FFN_FC):
        lo = c * FFN_FC
        gate = _dot(hn, wg_ref[:, lo:lo + FFN_FC])
        up = _dot(hn, wu_ref[:, lo:lo + FFN_FC])
        act = (_silu(gate) * up).astype(BF16)
        acc = acc + _dot(act, wd_ref[lo:lo + FFN_FC, :])
    y = x + 0.5 * acc
    if final_norm:
        y = _rms_norm(y, fin_ref[...])
    o_ref[...] = y


def _ffn_call(x2d, g, wg, wu, wd, fin, final_norm):
    t = x2d.shape[0]
    const = lambda i: (0, 0)
    return pl.pallas_call(
        functools.partial(_ffn_kernel, final_norm=final_norm),
        grid=(t // FFN_TM,),
        in_specs=[
            pl.BlockSpec((FFN_TM, D_MODEL), lambda i: (i, 0)),
            pl.BlockSpec((1, D_MODEL), const),
            pl.BlockSpec((D_MODEL, D_FF), const, pipeline_mode=pl.Buffered(1)),
            pl.BlockSpec((D_MODEL, D_FF), const, pipeline_mode=pl.Buffered(1)),
            pl.BlockSpec((D_FF, D_MODEL), const, pipeline_mode=pl.Buffered(1)),
            pl.BlockSpec((1, D_MODEL), const),
        ],
        out_specs=pl.BlockSpec((FFN_TM, D_MODEL), lambda i: (i, 0)),
        out_shape=jax.ShapeDtypeStruct(x2d.shape, F32),
        compiler_params=pltpu.CompilerParams(
            dimension_semantics=("arbitrary",), vmem_limit_bytes=VMEM_LIMIT_BYTES),
        name="ffn",
    )(x2d, g, wg, wu, wd, fin)


_C_POOL = 0
_C_GA = 256
_C_GG = 512
_C_K = 768
_C_END = _C_K + ATTN_HEADS * SLAB


def _shift_rows(buf_ref, rows, d):
    halo = buf_ref.shape[0] - rows
    return buf_ref[halo - d:halo - d + rows, :]


def _mixer_in_kernel(x_ref, g_ref, wm_ref, wqt_ref, wvt_ref, wzf_ref, fb_ref, pq_ref, pk_ref,
                     pw_ref, ps_ref, cw_ref, cb_ref, lg_ref, lb_ref,
                     ya_ref, qt_ref, k_ref, vt_ref, yc_ref,
                     pbuf, cbuf, sbuf, fcarry):
    i = pl.program_id(1)
    tm = MIX_TM

    @pl.when(i == 0)
    def _():
        pbuf[0:POOL_HALO, :] = jnp.zeros((POOL_HALO, POOL_WIDTH), F32)
        cbuf[0, 0:CONV_HALO, :] = jnp.zeros((CONV_HALO, CONV_CH), F32)
        sbuf[0:SCAN_HALO, :] = jnp.zeros((SCAN_HALO, LANES), F32)
        fcarry[...] = jnp.zeros((1, LANES), F32)

    hn = _rms_norm(x_ref[0], g_ref[...]).astype(BF16)

    z = _dot(hn, wzf_ref[...]) + fb_ref[...]
    logf = jnp.minimum(z, 0.0) - jnp.log1p(jnp.exp(-jnp.abs(z)))
    sbuf[SCAN_HALO:SCAN_HALO + tm, :] = logf
    d = 1
    cur = logf
    while d < tm:
        cur = sbuf[SCAN_HALO:SCAN_HALO + tm, :] + _shift_rows(sbuf, tm, d)
        sbuf[SCAN_HALO:SCAN_HALO + tm, :] = cur
        d *= 2
    ftile = cur + fcarry[...]
    fcarry[...] = ftile[tm - 1:tm, :]

    f2 = ftile * LOG2E
    hi = f2.astype(BF16).astype(F32)
    r1 = f2 - hi
    mid = r1.astype(BF16).astype(F32)
    lo = (r1 - mid).astype(BF16).astype(F32)
    lane = lax.broadcasted_iota(jnp.int32, (1, LANES), 1)
    fs = jnp.where(lane < ATTN_HEADS, hi,
                   jnp.where(lane < 2 * ATTN_HEADS, mid,
                             jnp.where(lane < FS_ONE, lo,
                                       jnp.where(lane == FS_ONE, 1.0, 0.0)))).astype(BF16)

    qt = lax.dot_general(wqt_ref[...], hn, NT_DIMS, preferred_element_type=F32) * (LOG2E / math.sqrt(HEAD_DIM))
    qt = qt + lax.dot_general(pq_ref[...], fs, NT_DIMS, preferred_element_type=F32)
    qt_ref[0] = qt.astype(BF16)
    ks = _dot(hn, wm_ref[:, _C_K:_C_END]) + _dot(fs, pk_ref[...])
    k_ref[0] = ks.astype(BF16)
    vt = lax.dot_general(wvt_ref[...], hn, NT_DIMS, preferred_element_type=F32).astype(BF16)
    for c in range(tm // ATT_KB):
        vt_ref[0, c] = vt[:, c * ATT_KB:(c + 1) * ATT_KB]

    u = _dot(hn, wm_ref[:, _C_POOL:_C_GA])
    rows = POOL_HALO + tm
    pbuf[POOL_HALO:rows, :] = u
    lane_p = lax.broadcasted_iota(jnp.int32, (1, POOL_WIDTH), 1)
    group = lane_p // POOL_GROUP_DIM
    ext = pbuf[...]
    sel = jnp.zeros((tm, POOL_WIDTH), F32)
    span = 1
    for gi, w in enumerate(POOL_WINDOWS):
        while span < w:
            ext = ext + pltpu.roll(ext, span, 0)
            span *= 2
        sel = jnp.where(group == gi, ext[POOL_HALO:, :], sel)
    win = jnp.zeros((1, POOL_WIDTH), F32)
    for gi, w in enumerate(POOL_WINDOWS):
        win = jnp.where(group == gi, float(w), win)
    pos = (i * tm + lax.broadcasted_iota(jnp.int32, (tm, 1), 0) + 1).astype(F32)
    count = jnp.minimum(pos, win)
    pooled = sel / count - u
    mixed = _dot(pooled.astype(BF16), pw_ref[...])
    ya_ref[0] = (mixed * ps_ref[...]).astype(BF16)
    pbuf[0:POOL_HALO, :] = pbuf[tm:tm + POOL_HALO, :]

    ga = _dot(hn, wm_ref[:, _C_GA:_C_GG])
    gg = _dot(hn, wm_ref[:, _C_GG:_C_K])
    crow = CONV_HALO + tm
    cbuf[0, CONV_HALO:crow, :] = ga * jax.nn.sigmoid(gg)
    base = cbuf[0]
    for r in range(1, SUBLANES):
        cbuf[r] = pltpu.roll(base, r, 0)
    cb = cb_ref[...]
    lg = lg_ref[...]
    lb = lb_ref[...]
    for rc in range(tm // CONV_ROWS):
        acc = jnp.broadcast_to(cb, (CONV_ROWS, CONV_CH))
        for j in range(CONV_KERNEL):
            delay = CONV_KERNEL - 1 - j
            r, a = delay % SUBLANES, delay // SUBLANES
            lo_row = CONV_HALO + rc * CONV_ROWS - a * SUBLANES
            acc = acc + cw_ref[j:j + 1, :] * cbuf[r, lo_row:lo_row + CONV_ROWS, :]
        mu = jnp.mean(acc, axis=-1, keepdims=True)
        xc = acc - mu
        var = jnp.mean(xc * xc, axis=-1, keepdims=True)
        yn = xc * lax.rsqrt(var + NORM_EPS) * lg + lb
        yc_ref[0, rc * CONV_ROWS:(rc + 1) * CONV_ROWS, :] = _silu(yn).astype(BF16)
    cbuf[0, 0:CONV_HALO, :] = cbuf[0, tm:tm + CONV_HALO, :]


def _mixer_in_call(x3d, g, wm, wqt, wvt, wzf, fb, pq, pk, pw, ps, cw, cb, lg, lb):
    b, s, _ = x3d.shape
    tm = MIX_TM
    const = lambda bi, i: (0, 0)
    row = lambda bi, i: (bi, i, 0)
    col = lambda bi, i: (bi, 0, i)
    slabs = ATTN_HEADS * SLAB
    out_shape = (
        jax.ShapeDtypeStruct((b, s, POOL_WIDTH), BF16),
        jax.ShapeDtypeStruct((b, slabs, s), BF16),
        jax.ShapeDtypeStruct((b, s, slabs), BF16),
        jax.ShapeDtypeStruct((b, s // ATT_KB, ATTN_WIDTH, ATT_KB), BF16),
        jax.ShapeDtypeStruct((b, s, CONV_CH), BF16),
    )
    return pl.pallas_call(
        _mixer_in_kernel,
        grid=(b, s // tm),
        in_specs=[
            pl.BlockSpec((1, tm, D_MODEL), row),
            pl.BlockSpec((1, D_MODEL), const),
            pl.BlockSpec((D_MODEL, _C_END), const, pipeline_mode=pl.Buffered(1)),
            pl.BlockSpec((slabs, D_MODEL), const, pipeline_mode=pl.Buffered(1)),
            pl.BlockSpec((ATTN_WIDTH, D_MODEL), const, pipeline_mode=pl.Buffered(1)),
            pl.BlockSpec((D_MODEL, LANES), const),
            pl.BlockSpec((1, LANES), const),
            pl.BlockSpec((slabs, LANES), const),
            pl.BlockSpec((LANES, slabs), const),
            pl.BlockSpec((POOL_WIDTH, POOL_WIDTH), const),
            pl.BlockSpec((1, POOL_WIDTH), const),
            pl.BlockSpec((CONV_HALO, CONV_CH), const),
            pl.BlockSpec((1, CONV_CH), const),
            pl.BlockSpec((1, CONV_CH), const),
            pl.BlockSpec((1, CONV_CH), const),
        ],
        out_specs=(
            pl.BlockSpec((1, tm, POOL_WIDTH), row),
            pl.BlockSpec((1, slabs, tm), col),
            pl.BlockSpec((1, tm, slabs), row),
            pl.BlockSpec((1, tm // ATT_KB, ATTN_WIDTH, ATT_KB), lambda bi, i: (bi, i, 0, 0)),
            pl.BlockSpec((1, tm, CONV_CH), row),
        ),
        out_shape=out_shape,
        scratch_shapes=[
            pltpu.VMEM((POOL_HALO + tm, POOL_WIDTH), F32),
            pltpu.VMEM((SUBLANES, CONV_HALO + tm, CONV_CH), F32),
            pltpu.VMEM((SCAN_HALO + tm, LANES), F32),
            pltpu.VMEM((1, LANES), F32),
        ],
        compiler_params=pltpu.CompilerParams(
            dimension_semantics=("arbitrary", "arbitrary"), vmem_limit_bytes=VMEM_LIMIT_BYTES),
        name="mixer_in",
    )(x3d, g, wm, wqt, wvt, wzf, fb, pq, pk, pw, ps, cw, cb, lg, lb)


_STREAMS = tuple((hh, qs) for hh in range(ATT_HG) for qs in range(ATT_TQ // ATT_QS))


def _attn_kernel(qt_ref, k_ref, vt_ref, o_ref, s0_buf, s1_buf, p0_buf, p1_buf, acc_buf):
    i = pl.program_id(2)
    kb, ck, qs_w = ATT_KB, ATT_CK, ATT_QS
    n_chunks = kb // ck
    s_bufs = (s0_buf, s1_buf)
    p_bufs = (p0_buf, p1_buf)

    def block_of(n):
        n = jnp.minimum(n, i)
        return jnp.where(n == 0, i, n - 1)

    def stage_a(n, par):
        row0 = pl.multiple_of(block_of(n) * kb, kb)
        for st, (hh, qs) in enumerate(_STREAMS):
            k_blk = k_ref[0, pl.ds(row0, kb), hh * SLAB:(hh + 1) * SLAB]
            qt = qt_ref[0, hh * SLAB:(hh + 1) * SLAB, qs * qs_w:(qs + 1) * qs_w]
            s_bufs[par][st] = _dot(k_blk, qt)

    def stage_b(par, ms, masked):
        new_ms, alphas = [], []
        for st, (hh, qs) in enumerate(_STREAMS):
            def chunk(c):
                sc = s_bufs[par][st, c * ck:(c + 1) * ck, :]
                if masked and (c + 1) * ck - 1 > qs * qs_w:
                    kv_pos = c * ck + lax.broadcasted_iota(jnp.int32, (ck, qs_w), 0)
                    q_pos = qs * qs_w + lax.broadcasted_iota(jnp.int32, (ck, qs_w), 1)
                    sc = jnp.where(kv_pos <= q_pos, sc, NEG_BIG)
                return sc

            bmax = jnp.max(chunk(0), axis=0, keepdims=True)
            for c in range(1, n_chunks):
                bmax = jnp.maximum(bmax, jnp.max(chunk(c), axis=0, keepdims=True))
            m_new = jnp.maximum(ms[st], bmax)
            alphas.append(jnp.exp2(ms[st] - m_new))
            new_ms.append(m_new)
            for c in range(n_chunks):
                p_bufs[par][st, c * ck:(c + 1) * ck, :] = jnp.exp2(chunk(c) - m_new).astype(BF16)
        return tuple(new_ms), tuple(alphas)

    ones_rows = jnp.ones((ATT_VROWS - HEAD_DIM, kb), BF16)

    def stage_c(n, par, alphas):
        jb = block_of(n)
        for st, (hh, qs) in enumerate(_STREAMS):
            v_blk = vt_ref[0, jb, hh * HEAD_DIM:(hh + 1) * HEAD_DIM, :]
            v_ext = jnp.concatenate([v_blk, ones_rows], axis=0)
            acc_buf[st] = alphas[st] * acc_buf[st] + _dot(v_ext, p_bufs[par][st])

    def tick(n, par, ms, alphas):
        stage_c(n - 1, 1 - par, alphas)
        new_ms, new_alphas = stage_b(par, ms, False)
        stage_a(n + 2, par)
        return new_ms, new_alphas

    for st in range(len(_STREAMS)):
        acc_buf[st] = jnp.zeros((ATT_VROWS, qs_w), F32)
    ms = tuple(jnp.full((1, qs_w), NEG_BIG, F32) for _ in _STREAMS)

    stage_a(0, 0)
    stage_a(1, 1)
    ms, alphas = stage_b(0, ms, True)
    stage_a(2, 0)

    def tick_pair(k, carry):
        n = 2 * k + 1
        carry = tick(n, 1, *carry)
        return tick(n + 1, 0, *carry)

    ms, alphas = lax.fori_loop(0, i // 2, tick_pair, (ms, alphas))
    i_odd = lax.rem(i, 2) == 1
    ms, alphas = lax.cond(i_odd, lambda _: tick(i, 1, ms, alphas), lambda _: (ms, alphas), 0)
    pl.when(i_odd)(lambda: stage_c(i, 1, alphas))
    pl.when(jnp.logical_not(i_odd))(lambda: stage_c(i, 0, alphas))
    for st, (hh, qs) in enumerate(_STREAMS):
        acc = acc_buf[st]
        out = acc[0:HEAD_DIM, :] / acc[HEAD_DIM:HEAD_DIM + 1, :]
        o_ref[0, hh * HEAD_DIM:(hh + 1) * HEAD_DIM, qs * qs_w:(qs + 1) * qs_w] = out.astype(BF16)


def _attn_call(qt, ks, vt):
    b, _, s = qt.shape
    tq = ATT_TQ
    n_groups = ATTN_HEADS // ATT_HG
    n_streams = len(_STREAMS)
    return pl.pallas_call(
        _attn_kernel,
        grid=(b, n_groups, s // tq),
        in_specs=[
            pl.BlockSpec((1, ATT_HG * SLAB, tq), lambda bi, hg, i: (bi, hg, i)),
            pl.BlockSpec((1, s, ATT_HG * SLAB), lambda bi, hg, i: (bi, 0, hg)),
            pl.BlockSpec((1, s // ATT_KB, ATT_HG * HEAD_DIM, ATT_KB), lambda bi, hg, i: (bi, 0, hg, 0)),
        ],
        out_specs=pl.BlockSpec((1, ATT_HG * HEAD_DIM, tq), lambda bi, hg, i: (bi, hg, i)),
        out_shape=jax.ShapeDtypeStruct((b, ATTN_WIDTH, s), BF16),
        scratch_shapes=[
            pltpu.VMEM((n_streams, ATT_KB, ATT_QS), F32),
            pltpu.VMEM((n_streams, ATT_KB, ATT_QS), F32),
            pltpu.VMEM((n_streams, ATT_KB, ATT_QS), BF16),
            pltpu.VMEM((n_streams, ATT_KB, ATT_QS), BF16),
            pltpu.VMEM((n_streams, ATT_VROWS, ATT_QS), F32),
        ],
        compiler_params=pltpu.CompilerParams(
            dimension_semantics=("arbitrary", "arbitrary", "arbitrary"), vmem_limit_bytes=VMEM_LIMIT_BYTES),
        name="attn",
    )(qt, ks, vt)


def _mixer_out_kernel(x_ref, ya_ref, ybt_ref, yc_ref, wo_ref, o_ref):
    a_end = POOL_WIDTH
    b_end = POOL_WIDTH + ATTN_WIDTH
    y = _dot(ya_ref[0], wo_ref[0:a_end, :])
    y = y + lax.dot_general(ybt_ref[0], wo_ref[a_end:b_end, :], TN_DIMS, preferred_element_type=F32)
    y = y + _dot(yc_ref[0], wo_ref[b_end:D_MODEL, :])
    o_ref[0] = x_ref[0] + y


def _mixer_out_call(x3d, ya, ybt, yc, wo):
    b, s, _ = x3d.shape
    tm = OUT_TM
    row = lambda bi, i: (bi, i, 0)
    return pl.pallas_call(
        _mixer_out_kernel,
        grid=(b, s // tm),
        in_specs=[
            pl.BlockSpec((1, tm, D_MODEL), row),
            pl.BlockSpec((1, tm, POOL_WIDTH), row),
            pl.BlockSpec((1, ATTN_WIDTH, tm), lambda bi, i: (bi, 0, i)),
            pl.BlockSpec((1, tm, CONV_CH), row),
            pl.BlockSpec((D_MODEL, D_MODEL), lambda bi, i: (0, 0)),
        ],
        out_specs=pl.BlockSpec((1, tm, D_MODEL), row),
        out_shape=jax.ShapeDtypeStruct(x3d.shape, F32),
        compiler_params=pltpu.CompilerParams(
            dimension_semantics=("arbitrary", "arbitrary"), vmem_limit_bytes=VMEM_LIMIT_BYTES),
        name="mixer_out",
    )(x3d, ya, ybt, yc, wo)


def _block_diag(pool_w):
    g, c, d = pool_w.shape
    out = jnp.zeros((g * c, g * d), pool_w.dtype)
    for gi in range(g):
        out = lax.dynamic_update_slice(out, pool_w[gi], (gi * c, gi * d))
    return out


def _head_slabs(w):
    dm = w.shape[0]
    w3 = w.reshape(dm, ATTN_HEADS, HEAD_DIM)
    return jnp.pad(w3, ((0, 0), (0, 0), (0, SLAB - HEAD_DIM))).reshape(dm, ATTN_HEADS * SLAB)


def _placement():
    pq = np.zeros((LANES, ATTN_HEADS * SLAB), np.float32)
    pk = np.zeros((LANES, ATTN_HEADS * SLAB), np.float32)
    for h in range(ATTN_HEADS):
        base = h * SLAB + HEAD_DIM
        for part in range(3):
            pq[part * ATTN_HEADS + h, base + part] = 1.0
            pq[FS_ONE, base + 3 + part] = 1.0
            pk[FS_ONE, base + part] = 1.0
            pk[part * ATTN_HEADS + h, base + 3 + part] = -1.0
    return jnp.asarray(pq.T, BF16), jnp.asarray(pk, BF16)


def _rep3(a):
    a3 = jnp.concatenate([a, a, a], axis=-1)
    return jnp.pad(a3, ((0, 0), (0, LANES - a3.shape[-1])))


def kernel(x, ffn1_norm, ffn1_w_gate, ffn1_w_up, ffn1_w_down, mix_norm, w_in, pool_w, pool_scale, forget_bias, conv_w, conv_b, conv_ln_g, conv_ln_b, w_out, ffn2_norm, ffn2_w_gate, ffn2_w_up, ffn2_w_down, final_norm):
    b, s, d = x.shape
    depth = w_in.shape[0]
    t = b * s
    x2d = x.reshape(t, d)
    fin = final_norm.reshape(1, d)
    c_q = POOL_WIDTH
    c_k = c_q + ATTN_WIDTH
    c_v = c_k + ATTN_WIDTH
    c_zf = c_v + ATTN_WIDTH
    c_glu = c_zf + ATTN_HEADS
    pq_t, pk = _placement()
    for l in range(depth):
        x2d = _ffn_call(x2d, ffn1_norm[l].reshape(1, d), ffn1_w_gate[l].astype(BF16), ffn1_w_up[l].astype(BF16),
                        ffn1_w_down[l].astype(BF16), fin, False)
        wl = w_in[l]
        wm = jnp.concatenate([wl[:, :c_q], wl[:, c_glu:], _head_slabs(wl[:, c_k:c_v])], axis=1).astype(BF16)
        wqt = _head_slabs(wl[:, c_q:c_k]).T.astype(BF16)
        wvt = wl[:, c_v:c_zf].T.astype(BF16)
        wzf = _rep3(wl[:, c_zf:c_glu]).astype(BF16)
        fb = _rep3(forget_bias[l].reshape(1, ATTN_HEADS))
        pw = _block_diag(pool_w[l]).astype(BF16)
        cw = jnp.pad(conv_w[l], ((0, CONV_HALO - CONV_KERNEL), (0, 0)))
        ya, qt, ks, vt, yc = _mixer_in_call(
            x2d.reshape(b, s, d), mix_norm[l].reshape(1, d), wm, wqt, wvt, wzf, fb, pq_t, pk, pw,
            pool_scale[l].reshape(1, POOL_WIDTH), cw, conv_b[l].reshape(1, CONV_CH),
            conv_ln_g[l].reshape(1, CONV_CH), conv_ln_b[l].reshape(1, CONV_CH))
        ybt = _attn_call(qt, ks, vt)
        x2d = _mixer_out_call(x2d.reshape(b, s, d), ya, ybt, yc, w_out[l].astype(BF16)).reshape(t, d)
        x2d = _ffn_call(x2d, ffn2_norm[l].reshape(1, d), ffn2_w_gate[l].astype(BF16), ffn2_w_up[l].astype(BF16),
                        ffn2_w_down[l].astype(BF16), fin, l == depth - 1)
    return x2d.reshape(b, s, d)
```

```python
import functools
import math

import numpy as np
import jax
import jax.numpy as jnp
from jax import lax
from jax.experimental import pallas as pl
from jax.experimental.pallas import tpu as pltpu

D_MODEL = 1024
D_FF = 2816
NORM_EPS = 1e-6
POOL_WINDOWS = (2, 4, 8, 16)
POOL_GROUP_DIM = 64
POOL_WIDTH = 256
ATTN_HEADS = 8
HEAD_DIM = 64
ATTN_WIDTH = 512
CONV_CH = 256
CONV_KERNEL = 31

LANES = 128
SUBLANES = 8
VMEM_LIMIT_BYTES = 56 * 1024 * 1024

FFN_TM = 512
FFN_FC = 256
MIX_TM = 512
POOL_HALO = 16
CONV_HALO = 32
SCAN_HALO = MIX_TM // 2
CONV_ROWS = 64
SLAB = LANES
ATT_TQ = 512
ATT_QS = 256
ATT_CK = 128
ATT_KB = 512
ATT_HG = 2
RING = 3
NEG_BIG = -1e30
LOG2E = math.log2(math.e)
ATT_VROWS = HEAD_DIM + 16

FS_ONE = 3 * ATTN_HEADS

BF16 = jnp.bfloat16
F32 = jnp.float32
NT_DIMS = (((1,), (1,)), ((), ()))
TN_DIMS = (((0,), (0,)), ((), ()))


def _rms_norm(x, g):
    ms = jnp.mean(x * x, axis=-1, keepdims=True)
    return x * lax.rsqrt(ms + NORM_EPS) * g


def _silu(x):
    return x * jax.nn.sigmoid(x)


def _dot(a, b):
    return jnp.dot(a, b, preferred_element_type=F32)


def _swiglu_residual(x, g_ref, wg_ref, wu_ref, wd_ref):
    hn = _rms_norm(x, g_ref[...]).astype(BF16)
    acc = jnp.zeros(x.shape, F32)
    for c in range(D_FF // FFN_FC):
        lo = c * FFN_FC
        gate = _dot(hn, wg_ref[:, lo:lo + FFN_FC])
        up = _dot(hn, wu_ref[:, lo:lo + FFN_FC])
        act = (_silu(gate) * up).astype(BF16)
        acc = acc + _dot(act, wd_ref[lo:lo + FFN_FC, :])
    return x + 0.5 * acc


def _ffn_kernel(x_ref, g_ref, wg_ref, wu_ref, wd_ref, o_ref):
    o_ref[...] = _swiglu_residual(x_ref[...], g_ref, wg_ref, wu_ref, wd_ref)


def _mix_ffn_kernel(x_ref, ya_ref, ybt_ref, yc_ref, wo_ref, g_ref, wg_ref, wu_ref, wd_ref, fin_ref, o_ref,
                    *, final_norm):
    a_end = POOL_WIDTH
    b_end = POOL_WIDTH + ATTN_WIDTH
    y = _dot(ya_ref[0], wo_ref[0:a_end, :])
    y = y + lax.dot_general(ybt_ref[0], wo_ref[a_end:b_end, :], TN_DIMS, preferred_element_type=F32)
    y = y + _dot(yc_ref[0], wo_ref[b_end:D_MODEL, :])
    out = _swiglu_residual(x_ref[0] + y, g_ref, wg_ref, wu_ref, wd_ref)
    if final_norm:
        out = _rms_norm(out, fin_ref[...])
    o_ref[0] = out


def _ffn_call(x2d, g, wg, wu, wd):
    t = x2d.shape[0]
    const = lambda i: (0, 0)
    return pl.pallas_call(
        _ffn_kernel,
        grid=(t // FFN_TM,),
        in_specs=[
            pl.BlockSpec((FFN_TM, D_MODEL), lambda i: (i, 0)),
            pl.BlockSpec((1, D_MODEL), const),
            pl.BlockSpec((D_MODEL, D_FF), const, pipeline_mode=pl.Buffered(1)),
            pl.BlockSpec((D_MODEL, D_FF), const, pipeline_mode=pl.Buffered(1)),
            pl.BlockSpec((D_FF, D_MODEL), const, pipeline_mode=pl.Buffered(1)),
        ],
        out_specs=pl.BlockSpec((FFN_TM, D_MODEL), lambda i: (i, 0)),
        out_shape=jax.ShapeDtypeStruct(x2d.shape, F32),
        compiler_params=pltpu.CompilerParams(
            dimension_semantics=("arbitrary",), vmem_limit_bytes=VMEM_LIMIT_BYTES),
        name="ffn",
    )(x2d, g, wg, wu, wd)


def _mix_ffn_call(x3d, ya, ybt, yc, wo, g, wg, wu, wd, fin, final_norm):
    b, s, _ = x3d.shape
    tm = FFN_TM
    row = lambda bi, i: (bi, i, 0)
    const = lambda bi, i: (0, 0)
    return pl.pallas_call(
        functools.partial(_mix_ffn_kernel, final_norm=final_norm),
        grid=(b, s // tm),
        in_specs=[
            pl.BlockSpec((1, tm, D_MODEL), row),
            pl.BlockSpec((1, tm, POOL_WIDTH), row),
            pl.BlockSpec((1, ATTN_WIDTH, tm), lambda bi, i: (bi, 0, i)),
            pl.BlockSpec((1, tm, CONV_CH), row),
            pl.BlockSpec((D_MODEL, D_MODEL), const, pipeline_mode=pl.Buffered(1)),
            pl.BlockSpec((1, D_MODEL), const),
            pl.BlockSpec((D_MODEL, D_FF), const, pipeline_mode=pl.Buffered(1)),
            pl.BlockSpec((D_MODEL, D_FF), const, pipeline_mode=pl.Buffered(1)),
            pl.BlockSpec((D_FF, D_MODEL), const, pipeline_mode=pl.Buffered(1)),
            pl.BlockSpec((1, D_MODEL), const),
        ],
        out_specs=pl.BlockSpec((1, tm, D_MODEL), row),
        out_shape=jax.ShapeDtypeStruct(x3d.shape, F32),
        compiler_params=pltpu.CompilerParams(
            dimension_semantics=("arbitrary", "arbitrary"), vmem_limit_bytes=VMEM_LIMIT_BYTES),
        name="mix_ffn",
    )(x3d, ya, ybt, yc, wo, g, wg, wu, wd, fin)


_C_POOL = 0
_C_GA = 256
_C_GG = 512
_C_K = 768
_C_END = _C_K + ATTN_HEADS * SLAB


def _shift_rows(buf_ref, rows, d):
    halo = buf_ref.shape[0] - rows
    return buf_ref[halo - d:halo - d + rows, :]


def _mixer_in_kernel(x_ref, g_ref, wm_ref, wqt_ref, wvt_ref, wzf_ref, fb_ref, pq_ref, pk_ref,
                     pw_ref, ps_ref, cw_ref, cb_ref, lg_ref, lb_ref,
                     ya_ref, qt_ref, k_ref, vt_ref, yc_ref,
                     pbuf, cbuf, sbuf, fcarry):
    i = pl.program_id(1)
    tm = MIX_TM

    @pl.when(i == 0)
    def _():
        pbuf[0:POOL_HALO, :] = jnp.zeros((POOL_HALO, POOL_WIDTH), F32)
        cbuf[0, 0:CONV_HALO, :] = jnp.zeros((CONV_HALO, CONV_CH), F32)
        sbuf[0:SCAN_HALO, :] = jnp.zeros((SCAN_HALO, LANES), F32)
        fcarry[...] = jnp.zeros((1, LANES), F32)

    hn = _rms_norm(x_ref[0], g_ref[...]).astype(BF16)

    z = _dot(hn, wzf_ref[...]) + fb_ref[...]
    ga = _dot(hn, wm_ref[:, _C_GA:_C_GG])
    gg = _dot(hn, wm_ref[:, _C_GG:_C_K])
    u = _dot(hn, wm_ref[:, _C_POOL:_C_GA])

    logf = jnp.minimum(z, 0.0) - jnp.log1p(jnp.exp(-jnp.abs(z)))
    sbuf[SCAN_HALO:SCAN_HALO + tm, :] = logf
    d = 1
    cur = logf
    while d < tm:
        cur = sbuf[SCAN_HALO:SCAN_HALO + tm, :] + _shift_rows(sbuf, tm, d)
        sbuf[SCAN_HALO:SCAN_HALO + tm, :] = cur
        d *= 2
    ftile = cur + fcarry[...]
    fcarry[...] = ftile[tm - 1:tm, :]

    f2 = ftile * LOG2E
    hi = f2.astype(BF16).astype(F32)
    r1 = f2 - hi
    mid = r1.astype(BF16).astype(F32)
    lo = (r1 - mid).astype(BF16).astype(F32)
    lane = lax.broadcasted_iota(jnp.int32, (1, LANES), 1)
    fs = jnp.where(lane < ATTN_HEADS, hi,
                   jnp.where(lane < 2 * ATTN_HEADS, mid,
                             jnp.where(lane < FS_ONE, lo,
                                       jnp.where(lane == FS_ONE, 1.0, 0.0)))).astype(BF16)

    rows = POOL_HALO + tm
    pbuf[POOL_HALO:rows, :] = u
    lane_p = lax.broadcasted_iota(jnp.int32, (1, POOL_WIDTH), 1)
    group = lane_p // POOL_GROUP_DIM
    ext = pbuf[...]
    sel = jnp.zeros((tm, POOL_WIDTH), F32)
    span = 1
    for gi, w in enumerate(POOL_WINDOWS):
        while span < w:
            ext = ext + pltpu.roll(ext, span, 0)
            span *= 2
        sel = jnp.where(group == gi, ext[POOL_HALO:, :], sel)
    win = jnp.zeros((1, POOL_WIDTH), F32)
    for gi, w in enumerate(POOL_WINDOWS):
        win = jnp.where(group == gi, float(w), win)
    pos = (i * tm + lax.broadcasted_iota(jnp.int32, (tm, 1), 0) + 1).astype(F32)
    count = jnp.minimum(pos, win)
    pooled = (sel / count - u).astype(BF16)
    pbuf[0:POOL_HALO, :] = pbuf[tm:tm + POOL_HALO, :]

    crow = CONV_HALO + tm
    cbuf[0, CONV_HALO:crow, :] = ga * jax.nn.sigmoid(gg)
    base = cbuf[0]
    for r in range(1, SUBLANES):
        cbuf[r] = pltpu.roll(base, r, 0)
    cb = cb_ref[...]
    lg = lg_ref[...]
    lb = lb_ref[...]
    for rc in range(tm // CONV_ROWS):
        acc = jnp.broadcast_to(cb, (CONV_ROWS, CONV_CH))
        for j in range(CONV_KERNEL):
            delay = CONV_KERNEL - 1 - j
            r, a = delay % SUBLANES, delay // SUBLANES
            lo_row = CONV_HALO + rc * CONV_ROWS - a * SUBLANES
            acc = acc + cw_ref[j:j + 1, :] * cbuf[r, lo_row:lo_row + CONV_ROWS, :]
        mu = jnp.mean(acc, axis=-1, keepdims=True)
        xc = acc - mu
        var = jnp.mean(xc * xc, axis=-1, keepdims=True)
        yn = xc * lax.rsqrt(var + NORM_EPS) * lg + lb
        yc_ref[0, rc * CONV_ROWS:(rc + 1) * CONV_ROWS, :] = _silu(yn).astype(BF16)
    cbuf[0, 0:CONV_HALO, :] = cbuf[0, tm:tm + CONV_HALO, :]

    qt = lax.dot_general(wqt_ref[...], hn, NT_DIMS, preferred_element_type=F32) * (LOG2E / math.sqrt(HEAD_DIM))
    ks = _dot(hn, wm_ref[:, _C_K:_C_END])
    vt = lax.dot_general(wvt_ref[...], hn, NT_DIMS, preferred_element_type=F32).astype(BF16)
    for c in range(tm // ATT_KB):
        vt_ref[0, c] = vt[:, c * ATT_KB:(c + 1) * ATT_KB]
    qt = qt + lax.dot_general(pq_ref[...], fs, NT_DIMS, preferred_element_type=F32)
    qt_ref[0, 0] = qt.astype(BF16)
    k_ref[0] = (ks + _dot(fs, pk_ref[...])).astype(BF16)
    ya_ref[0] = (_dot(pooled, pw_ref[...]) * ps_ref[...]).astype(BF16)


def _mixer_in_call(x3d, g, wm, wqt, wvt, wzf, fb, pq, pk, pw, ps, cw, cb, lg, lb):
    b, s, _ = x3d.shape
    tm = MIX_TM
    assert tm == ATT_TQ == ATT_KB
    const = lambda bi, i: (0, 0)
    row = lambda bi, i: (bi, i, 0)
    slabs = ATTN_HEADS * SLAB
    out_shape = (
        jax.ShapeDtypeStruct((b, s, POOL_WIDTH), BF16),
        jax.ShapeDtypeStruct((b, s // ATT_TQ, slabs, ATT_TQ), BF16),
        jax.ShapeDtypeStruct((b, s, slabs), BF16),
        jax.ShapeDtypeStruct((b, s // ATT_KB, ATTN_WIDTH, ATT_KB), BF16),
        jax.ShapeDtypeStruct((b, s, CONV_CH), BF16),
    )
    return pl.pallas_call(
        _mixer_in_kernel,
        grid=(b, s // tm),
        in_specs=[
            pl.BlockSpec((1, tm, D_MODEL), row),
            pl.BlockSpec((1, D_MODEL), const),
            pl.BlockSpec((D_MODEL, _C_END), const, pipeline_mode=pl.Buffered(1)),
            pl.BlockSpec((slabs, D_MODEL), const, pipeline_mode=pl.Buffered(1)),
            pl.BlockSpec((ATTN_WIDTH, D_MODEL), const, pipeline_mode=pl.Buffered(1)),
            pl.BlockSpec((D_MODEL, LANES), const),
            pl.BlockSpec((1, LANES), const),
            pl.BlockSpec((slabs, LANES), const),
            pl.BlockSpec((LANES, slabs), const),
            pl.BlockSpec((POOL_WIDTH, POOL_WIDTH), const),
            pl.BlockSpec((1, POOL_WIDTH), const),
            pl.BlockSpec((CONV_HALO, CONV_CH), const),
            pl.BlockSpec((1, CONV_CH), const),
            pl.BlockSpec((1, CONV_CH), const),
            pl.BlockSpec((1, CONV_CH), const),
        ],
        out_specs=(
            pl.BlockSpec((1, tm, POOL_WIDTH), row),
            pl.BlockSpec((1, 1, slabs, ATT_TQ), lambda bi, i: (bi, i, 0, 0)),
            pl.BlockSpec((1, tm, slabs), row),
            pl.BlockSpec((1, tm // ATT_KB, ATTN_WIDTH, ATT_KB), lambda bi, i: (bi, i, 0, 0)),
            pl.BlockSpec((1, tm, CONV_CH), row),
        ),
        out_shape=out_shape,
        scratch_shapes=[
            pltpu.VMEM((POOL_HALO + tm, POOL_WIDTH), F32),
            pltpu.VMEM((SUBLANES, CONV_HALO + tm, CONV_CH), F32),
            pltpu.VMEM((SCAN_HALO + tm, LANES), F32),
            pltpu.VMEM((1, LANES), F32),
        ],
        compiler_params=pltpu.CompilerParams(
            dimension_semantics=("arbitrary", "arbitrary"), vmem_limit_bytes=VMEM_LIMIT_BYTES),
        name="mixer_in",
    )(x3d, g, wm, wqt, wvt, wzf, fb, pq, pk, pw, ps, cw, cb, lg, lb)


_STREAMS = tuple((hh, qs) for hh in range(ATT_HG) for qs in range(ATT_TQ // ATT_QS))


def _attn_kernel(qt_ref, k_ref, vt_ref, o_ref, *scratch):
    i = pl.program_id(2)
    kb, ck, qs_w = ATT_KB, ATT_CK, ATT_QS
    n_chunks = kb // ck
    s_bufs = scratch[0:RING]
    p_bufs = scratch[RING:2 * RING]
    acc_buf = scratch[2 * RING]

    def kv_block(q_blk, n):
        return jnp.where(n == 0, q_blk, n - 1)

    def stage_a(n, par):
        ahead = n > i
        q_blk = jnp.where(ahead, jnp.minimum(i + 1, pl.num_programs(2) - 1), i)
        row0 = pl.multiple_of(kv_block(q_blk, jnp.where(ahead, n - (i + 1), n)) * kb, kb)
        for st, (hh, qs) in enumerate(_STREAMS):
            k_blk = k_ref[0, pl.ds(row0, kb), hh * SLAB:(hh + 1) * SLAB]
            qt = qt_ref[0, q_blk, hh * SLAB:(hh + 1) * SLAB, qs * qs_w:(qs + 1) * qs_w]
            s_bufs[par][st] = _dot(k_blk, qt)

    def stage_b(par, ms, masked):
        new_ms, alphas = [], []
        for st, (hh, qs) in enumerate(_STREAMS):
            def chunk(c):
                sc = s_bufs[par][st, c * ck:(c + 1) * ck, :]
                if masked and (c + 1) * ck - 1 > qs * qs_w:
                    kv_pos = c * ck + lax.broadcasted_iota(jnp.int32, (ck, qs_w), 0)
                    q_pos = qs * qs_w + lax.broadcasted_iota(jnp.int32, (ck, qs_w), 1)
                    sc = jnp.where(kv_pos <= q_pos, sc, NEG_BIG)
                return sc

            bmax = jnp.max(chunk(0), axis=0, keepdims=True)
            for c in range(1, n_chunks):
                bmax = jnp.maximum(bmax, jnp.max(chunk(c), axis=0, keepdims=True))
            m_new = jnp.maximum(ms[st], bmax)
            alphas.append(jnp.exp2(ms[st] - m_new))
            new_ms.append(m_new)
            for c in range(n_chunks):
                p_bufs[par][st, c * ck:(c + 1) * ck, :] = jnp.exp2(chunk(c) - m_new).astype(BF16)
        return tuple(new_ms), tuple(alphas)

    ones_rows = jnp.ones((ATT_VROWS - HEAD_DIM, kb), BF16)

    def stage_c(n, par, alphas):
        jb = kv_block(i, n)
        for st, (hh, qs) in enumerate(_STREAMS):
            v_blk = vt_ref[0, jb, hh * HEAD_DIM:(hh + 1) * HEAD_DIM, :]
            v_ext = jnp.concatenate([v_blk, ones_rows], axis=0)
            acc_buf[st] = alphas[st] * acc_buf[st] + _dot(v_ext, p_bufs[par][st])

    def tick(n, slot, ms, al1, al2, with_c=True):
        stage_a(n + 2, (slot + 2) % RING)
        if with_c:
            stage_c(n - 2, (slot + 1) % RING, al2)
        new_ms, new_al = stage_b(slot, ms, False)
        return new_ms, new_al, al1

    def run_q_block(off):
        slot_of = lambda n: (n + off) % RING
        for st in range(len(_STREAMS)):
            acc_buf[st] = jnp.zeros((ATT_VROWS, qs_w), F32)
        ms = tuple(jnp.full((1, qs_w), NEG_BIG, F32) for _ in _STREAMS)

        if off == 0:
            @pl.when(i == 0)
            def _():
                stage_a(0, slot_of(0))
                stage_a(1, slot_of(1))

        ms, al1 = stage_b(slot_of(0), ms, True)
        stage_a(2, slot_of(2))
        state = (ms, al1, al1)
        state = lax.cond(i >= 1, lambda _: tick(1, slot_of(1), *state, with_c=False), lambda _: state, 0)

        def tick_ring(k, carry):
            n = RING * k + 2
            for r in range(RING):
                carry = tick(n + r, slot_of(2 + r), *carry)
            return carry

        n_ring = jnp.maximum(i - 1, 0)
        state = lax.fori_loop(0, n_ring // RING, tick_ring, state)
        left = lax.rem(n_ring, RING)
        state = lax.cond(left >= 1, lambda _: tick(i - left + 1, slot_of(2), *state), lambda _: state, 0)
        state = lax.cond(left == 2, lambda _: tick(i, slot_of(0), *state), lambda _: state, 0)
        _, al1, al2 = state
        for slot in range(RING):
            pl.when(jnp.logical_and(i >= 1, lax.rem(i + off + RING - 1, RING) == slot))(
                functools.partial(stage_c, i - 1, slot, al2))
        for slot in range(RING):
            pl.when(lax.rem(i + off, RING) == slot)(functools.partial(stage_c, i, slot, al1))

    off_is_one = lax.rem(i, RING) == 1
    pl.when(off_is_one)(functools.partial(run_q_block, 1))
    pl.when(jnp.logical_not(off_is_one))(functools.partial(run_q_block, 0))
    for st, (hh, qs) in enumerate(_STREAMS):
        acc = acc_buf[st]
        out = acc[0:HEAD_DIM, :] / acc[HEAD_DIM:HEAD_DIM + 1, :]
        o_ref[0, hh * HEAD_DIM:(hh + 1) * HEAD_DIM, qs * qs_w:(qs + 1) * qs_w] = out.astype(BF16)


def _attn_call(qt, ks, vt):
    b, s, _ = ks.shape
    tq = ATT_TQ
    n_groups = ATTN_HEADS // ATT_HG
    n_streams = len(_STREAMS)
    return pl.pallas_call(
        _attn_kernel,
        grid=(b, n_groups, s // tq),
        in_specs=[
            pl.BlockSpec((1, s // tq, ATT_HG * SLAB, tq), lambda bi, hg, i: (bi, 0, hg, 0)),
            pl.BlockSpec((1, s, ATT_HG * SLAB), lambda bi, hg, i: (bi, 0, hg)),
            pl.BlockSpec((1, s // ATT_KB, ATT_HG * HEAD_DIM, ATT_KB), lambda bi, hg, i: (bi, 0, hg, 0)),
        ],
        out_specs=pl.BlockSpec((1, ATT_HG * HEAD_DIM, tq), lambda bi, hg, i: (bi, hg, i)),
        out_shape=jax.ShapeDtypeStruct((b, ATTN_WIDTH, s), BF16),
        scratch_shapes=(
            [pltpu.VMEM((n_streams, ATT_KB, ATT_QS), F32)] * RING
            + [pltpu.VMEM((n_streams, ATT_KB, ATT_QS), BF16)] * RING
            + [pltpu.VMEM((n_streams, ATT_VROWS, ATT_QS), F32)]),
        compiler_params=pltpu.CompilerParams(
            dimension_semantics=("arbitrary", "arbitrary", "arbitrary"), vmem_limit_bytes=VMEM_LIMIT_BYTES),
        name="attn",
    )(qt, ks, vt)


def _block_diag(pool_w):
    g, c, d = pool_w.shape
    out = jnp.zeros((g * c, g * d), pool_w.dtype)
    for gi in range(g):
        out = lax.dynamic_update_slice(out, pool_w[gi], (gi * c, gi * d))
    return out


def _head_slabs(w):
    dm = w.shape[0]
    w3 = w.reshape(dm, ATTN_HEADS, HEAD_DIM)
    return jnp.pad(w3, ((0, 0), (0, 0), (0, SLAB - HEAD_DIM))).reshape(dm, ATTN_HEADS * SLAB)


def _placement():
    pq = np.zeros((LANES, ATTN_HEADS * SLAB), np.float32)
    pk = np.zeros((LANES, ATTN_HEADS * SLAB), np.float32)
    for h in range(ATTN_HEADS):
        base = h * SLAB + HEAD_DIM
        for part in range(3):
            pq[part * ATTN_HEADS + h, base + part] = 1.0
            pq[FS_ONE, base + 3 + part] = 1.0
            pk[FS_ONE, base + part] = 1.0
            pk[part * ATTN_HEADS + h, base + 3 + part] = -1.0
    return jnp.asarray(pq.T, BF16), jnp.asarray(pk, BF16)


def _rep3(a):
    a3 = jnp.concatenate([a, a, a], axis=-1)
    return jnp.pad(a3, ((0, 0), (0, LANES - a3.shape[-1])))


def kernel(x, ffn1_norm, ffn1_w_gate, ffn1_w_up, ffn1_w_down, mix_norm, w_in, pool_w, pool_scale, forget_bias, conv_w, conv_b, conv_ln_g, conv_ln_b, w_out, ffn2_norm, ffn2_w_gate, ffn2_w_up, ffn2_w_down, final_norm):
    b, s, d = x.shape
    depth = w_in.shape[0]
    t = b * s
    x2d = x.reshape(t, d)
    fin = final_norm.reshape(1, d)
    c_q = POOL_WIDTH
    c_k = c_q + ATTN_WIDTH
    c_v = c_k + ATTN_WIDTH
    c_zf = c_v + ATTN_WIDTH
    c_glu = c_zf + ATTN_HEADS
    pq_t, pk = _placement()
    for l in range(depth):
        x2d = _ffn_call(x2d, ffn1_norm[l].reshape(1, d), ffn1_w_gate[l].astype(BF16), ffn1_w_up[l].astype(BF16),
                        ffn1_w_down[l].astype(BF16))
        wl = w_in[l]
        wm = jnp.concatenate([wl[:, :c_q], wl[:, c_glu:], _head_slabs(wl[:, c_k:c_v])], axis=1).astype(BF16)
        wqt = _head_slabs(wl[:, c_q:c_k]).T.astype(BF16)
        wvt = wl[:, c_v:c_zf].T.astype(BF16)
        wzf = _rep3(wl[:, c_zf:c_glu]).astype(BF16)
        fb = _rep3(forget_bias[l].reshape(1, ATTN_HEADS))
        pw = _block_diag(pool_w[l]).astype(BF16)
        cw = jnp.pad(conv_w[l], ((0, CONV_HALO - CONV_KERNEL), (0, 0)))
        ya, qt, ks, vt, yc = _mixer_in_call(
            x2d.reshape(b, s, d), mix_norm[l].reshape(1, d), wm, wqt, wvt, wzf, fb, pq_t, pk, pw,
            pool_scale[l].reshape(1, POOL_WIDTH), cw, conv_b[l].reshape(1, CONV_CH),
            conv_ln_g[l].reshape(1, CONV_CH), conv_ln_b[l].reshape(1, CONV_CH))
        ybt = _attn_call(qt, ks, vt)
        x2d = _mix_ffn_call(x2d.reshape(b, s, d), ya, ybt, yc, w_out[l].astype(BF16),
                            ffn2_norm[l].reshape(1, d), ffn2_w_gate[l].astype(BF16), ffn2_w_up[l].astype(BF16),
                            ffn2_w_down[l].astype(BF16), fin, l == depth - 1).reshape(t, d)
    return x2d.reshape(b, s, d)
```

```python
import functools
import math

import numpy as np
import jax
import jax.numpy as jnp
from jax import lax
from jax.experimental import pallas as pl
from jax.experimental.pallas import tpu as pltpu

D_MODEL = 1024
D_FF = 2816
NORM_EPS = 1e-6
POOL_WINDOWS = (2, 4, 8, 16)
POOL_GROUP_DIM = 64
POOL_WIDTH = 256
ATTN_HEADS = 8
HEAD_DIM = 64
ATTN_WIDTH = 512
CONV_CH = 256
CONV_KERNEL = 31

LANES = 128
SUBLANES = 8
VMEM_LIMIT_BYTES = 56 * 1024 * 1024

FFN_TM = 512
FFN_FC = 256
MIX_TM = 512
POOL_HALO = 16
CONV_HALO = 32
SCAN_HALO = MIX_TM // 2
CONV_ROWS = 64
SLAB = LANES
ATT_TQ = 512
ATT_QS = 256
ATT_CK = 128
ATT_KB = 512
ATT_HG = 2
RING = 3
NEG_BIG = -1e30
LOG2E = math.log2(math.e)
ATT_VROWS = HEAD_DIM + 16

FS_ONE = 3 * ATTN_HEADS

BF16 = jnp.bfloat16
F32 = jnp.float32
NT_DIMS = (((1,), (1,)), ((), ()))
TN_DIMS = (((0,), (0,)), ((), ()))


def _rms_norm(x, g):
    ms = jnp.mean(x * x, axis=-1, keepdims=True)
    return x * lax.rsqrt(ms + NORM_EPS) * g


def _silu(x):
    return x * jax.nn.sigmoid(x)


def _dot(a, b):
    return jnp.dot(a, b, preferred_element_type=F32)


def _swiglu_residual(x, g_ref, wg_ref, wu_ref, wd_ref):
    hn = _rms_norm(x, g_ref[...]).astype(BF16)
    acc = jnp.zeros(x.shape, F32)
    for c in range(D_FF // FFN_FC):
        lo = c * FFN_FC
        gate = _dot(hn, wg_ref[:, lo:lo + FFN_FC])
        up = _dot(hn, wu_ref[:, lo:lo + FFN_FC])
        act = (_silu(gate) * up).astype(BF16)
        acc = acc + _dot(act, wd_ref[lo:lo + FFN_FC, :])
    return x + 0.5 * acc


def _ffn_kernel(x_ref, g_ref, wg_ref, wu_ref, wd_ref, o_ref):
    o_ref[...] = _swiglu_residual(x_ref[...], g_ref, wg_ref, wu_ref, wd_ref)


def _mix_ffn_kernel(x_ref, ya_ref, ybt_ref, yc_ref, wo_ref, g_ref, wg_ref, wu_ref, wd_ref, fin_ref, o_ref,
                    *, final_norm):
    a_end = POOL_WIDTH
    b_end = POOL_WIDTH + ATTN_WIDTH
    y = _dot(ya_ref[0], wo_ref[0:a_end, :])
    y = y + lax.dot_general(ybt_ref[0], wo_ref[a_end:b_end, :], TN_DIMS, preferred_element_type=F32)
    y = y + _dot(yc_ref[0], wo_ref[b_end:D_MODEL, :])
    out = _swiglu_residual(x_ref[0] + y, g_ref, wg_ref, wu_ref, wd_ref)
    if final_norm:
        out = _rms_norm(out, fin_ref[...])
    o_ref[0] = out


def _ffn_call(x2d, g, wg, wu, wd):
    t = x2d.shape[0]
    const = lambda i: (0, 0)
    return pl.pallas_call(
        _ffn_kernel,
        grid=(t // FFN_TM,),
        in_specs=[
            pl.BlockSpec((FFN_TM, D_MODEL), lambda i: (i, 0)),
            pl.BlockSpec((1, D_MODEL), const),
            pl.BlockSpec((D_MODEL, D_FF), const, pipeline_mode=pl.Buffered(1)),
            pl.BlockSpec((D_MODEL, D_FF), const, pipeline_mode=pl.Buffered(1)),
            pl.BlockSpec((D_FF, D_MODEL), const, pipeline_mode=pl.Buffered(1)),
        ],
        out_specs=pl.BlockSpec((FFN_TM, D_MODEL), lambda i: (i, 0)),
        out_shape=jax.ShapeDtypeStruct(x2d.shape, F32),
        compiler_params=pltpu.CompilerParams(
            dimension_semantics=("arbitrary",), vmem_limit_bytes=VMEM_LIMIT_BYTES),
        name="ffn",
    )(x2d, g, wg, wu, wd)


def _mix_ffn_call(x3d, ya, ybt, yc, wo, g, wg, wu, wd, fin, final_norm):
    b, s, _ = x3d.shape
    tm = FFN_TM
    row = lambda bi, i: (bi, i, 0)
    const = lambda bi, i: (0, 0)
    return pl.pallas_call(
        functools.partial(_mix_ffn_kernel, final_norm=final_norm),
        grid=(b, s // tm),
        in_specs=[
            pl.BlockSpec((1, tm, D_MODEL), row),
            pl.BlockSpec((1, tm, POOL_WIDTH), row),
            pl.BlockSpec((1, ATTN_WIDTH, tm), lambda bi, i: (bi, 0, i)),
            pl.BlockSpec((1, tm, CONV_CH), row),
            pl.BlockSpec((D_MODEL, D_MODEL), const, pipeline_mode=pl.Buffered(1)),
            pl.BlockSpec((1, D_MODEL), const),
            pl.BlockSpec((D_MODEL, D_FF), const, pipeline_mode=pl.Buffered(1)),
            pl.BlockSpec((D_MODEL, D_FF), const, pipeline_mode=pl.Buffered(1)),
            pl.BlockSpec((D_FF, D_MODEL), const, pipeline_mode=pl.Buffered(1)),
            pl.BlockSpec((1, D_MODEL), const),
        ],
        out_specs=pl.BlockSpec((1, tm, D_MODEL), row),
        out_shape=jax.ShapeDtypeStruct(x3d.shape, F32),
        compiler_params=pltpu.CompilerParams(
            dimension_semantics=("arbitrary", "arbitrary"), vmem_limit_bytes=VMEM_LIMIT_BYTES),
        name="mix_ffn",
    )(x3d, ya, ybt, yc, wo, g, wg, wu, wd, fin)


_C_POOL = 0
_C_GA = 256
_C_GG = 512
_C_K = 768
_C_END = _C_K + ATTN_HEADS * SLAB


def _shift_rows(buf_ref, rows, d):
    halo = buf_ref.shape[0] - rows
    return buf_ref[halo - d:halo - d + rows, :]


def _mixer_in_kernel(x_ref, g_ref, wm_ref, wqt_ref, wvt_ref, wzf_ref, fb_ref, pq_ref, pk_ref,
                     pw_ref, ps_ref, cw_ref, cb_ref, lg_ref, lb_ref,
                     ya_ref, qt_ref, k_ref, vt_ref, yc_ref,
                     pbuf, cbuf, sbuf, fcarry):
    i = pl.program_id(1)
    tm = MIX_TM

    @pl.when(i == 0)
    def _():
        pbuf[0:POOL_HALO, :] = jnp.zeros((POOL_HALO, POOL_WIDTH), F32)
        cbuf[0, 0:CONV_HALO, :] = jnp.zeros((CONV_HALO, CONV_CH), F32)
        sbuf[0:SCAN_HALO, :] = jnp.zeros((SCAN_HALO, LANES), F32)
        fcarry[...] = jnp.zeros((1, LANES), F32)

    hn = _rms_norm(x_ref[0], g_ref[...]).astype(BF16)

    z = _dot(hn, wzf_ref[...]) + fb_ref[...]
    ga = _dot(hn, wm_ref[:, _C_GA:_C_GG])
    gg = _dot(hn, wm_ref[:, _C_GG:_C_K])
    u = _dot(hn, wm_ref[:, _C_POOL:_C_GA])

    logf = jnp.minimum(z, 0.0) - jnp.log1p(jnp.exp(-jnp.abs(z)))
    sbuf[SCAN_HALO:SCAN_HALO + tm, :] = logf
    d = 1
    cur = logf
    while d < tm:
        cur = sbuf[SCAN_HALO:SCAN_HALO + tm, :] + _shift_rows(sbuf, tm, d)
        sbuf[SCAN_HALO:SCAN_HALO + tm, :] = cur
        d *= 2
    ftile = cur + fcarry[...]
    fcarry[...] = ftile[tm - 1:tm, :]

    f2 = ftile * LOG2E
    hi = f2.astype(BF16).astype(F32)
    r1 = f2 - hi
    mid = r1.astype(BF16).astype(F32)
    lo = (r1 - mid).astype(BF16).astype(F32)
    lane = lax.broadcasted_iota(jnp.int32, (1, LANES), 1)
    fs = jnp.where(lane < ATTN_HEADS, hi,
                   jnp.where(lane < 2 * ATTN_HEADS, mid,
                             jnp.where(lane < FS_ONE, lo,
                                       jnp.where(lane == FS_ONE, 1.0, 0.0)))).astype(BF16)

    rows = POOL_HALO + tm
    pbuf[POOL_HALO:rows, :] = u
    lane_p = lax.broadcasted_iota(jnp.int32, (1, POOL_WIDTH), 1)
    group = lane_p // POOL_GROUP_DIM
    ext = pbuf[...]
    sel = jnp.zeros((tm, POOL_WIDTH), F32)
    span = 1
    for gi, w in enumerate(POOL_WINDOWS):
        while span < w:
            ext = ext + pltpu.roll(ext, span, 0)
            span *= 2
        sel = jnp.where(group == gi, ext[POOL_HALO:, :], sel)
    win = jnp.zeros((1, POOL_WIDTH), F32)
    for gi, w in enumerate(POOL_WINDOWS):
        win = jnp.where(group == gi, float(w), win)
    pos = (i * tm + lax.broadcasted_iota(jnp.int32, (tm, 1), 0) + 1).astype(F32)
    count = jnp.minimum(pos, win)
    pooled = (sel / count - u).astype(BF16)
    pbuf[0:POOL_HALO, :] = pbuf[tm:tm + POOL_HALO, :]

    crow = CONV_HALO + tm
    cbuf[0, CONV_HALO:crow, :] = ga * jax.nn.sigmoid(gg)
    base = cbuf[0]
    for r in range(1, SUBLANES):
        cbuf[r] = pltpu.roll(base, r, 0)
    cb = cb_ref[...]
    lg = lg_ref[...]
    lb = lb_ref[...]
    for rc in range(tm // CONV_ROWS):
        acc = jnp.broadcast_to(cb, (CONV_ROWS, CONV_CH))
        for j in range(CONV_KERNEL):
            delay = CONV_KERNEL - 1 - j
            r, a = delay % SUBLANES, delay // SUBLANES
            lo_row = CONV_HALO + rc * CONV_ROWS - a * SUBLANES
            acc = acc + cw_ref[j:j + 1, :] * cbuf[r, lo_row:lo_row + CONV_ROWS, :]
        mu = jnp.mean(acc, axis=-1, keepdims=True)
        xc = acc - mu
        var = jnp.mean(xc * xc, axis=-1, keepdims=True)
        yn = xc * lax.rsqrt(var + NORM_EPS) * lg + lb
        yc_ref[0, rc * CONV_ROWS:(rc + 1) * CONV_ROWS, :] = _silu(yn).astype(BF16)
    cbuf[0, 0:CONV_HALO, :] = cbuf[0, tm:tm + CONV_HALO, :]

    qt = lax.dot_general(wqt_ref[...], hn, NT_DIMS, preferred_element_type=F32) * (LOG2E / math.sqrt(HEAD_DIM))
    ks = _dot(hn, wm_ref[:, _C_K:_C_END])
    vt = lax.dot_general(wvt_ref[...], hn, NT_DIMS, preferred_element_type=F32).astype(BF16)
    for c in range(tm // ATT_KB):
        vt_ref[0, c] = vt[:, c * ATT_KB:(c + 1) * ATT_KB]
    qt = qt + lax.dot_general(pq_ref[...], fs, NT_DIMS, preferred_element_type=F32)
    qt_ref[0, 0] = qt.astype(BF16)
    k_ref[0] = (ks + _dot(fs, pk_ref[...])).astype(BF16)
    ya_ref[0] = (_dot(pooled, pw_ref[...]) * ps_ref[...]).astype(BF16)


def _mixer_in_call(x3d, g, wm, wqt, wvt, wzf, fb, pq, pk, pw, ps, cw, cb, lg, lb):
    b, s, _ = x3d.shape
    tm = MIX_TM
    assert tm == ATT_TQ == ATT_KB
    const = lambda bi, i: (0, 0)
    row = lambda bi, i: (bi, i, 0)
    slabs = ATTN_HEADS * SLAB
    out_shape = (
        jax.ShapeDtypeStruct((b, s, POOL_WIDTH), BF16),
        jax.ShapeDtypeStruct((b, s // ATT_TQ, slabs, ATT_TQ), BF16),
        jax.ShapeDtypeStruct((b, s, slabs), BF16),
        jax.ShapeDtypeStruct((b, s // ATT_KB, ATTN_WIDTH, ATT_KB), BF16),
        jax.ShapeDtypeStruct((b, s, CONV_CH), BF16),
    )
    return pl.pallas_call(
        _mixer_in_kernel,
        grid=(b, s // tm),
        in_specs=[
            pl.BlockSpec((1, tm, D_MODEL), row),
            pl.BlockSpec((1, D_MODEL), const),
            pl.BlockSpec((D_MODEL, _C_END), const, pipeline_mode=pl.Buffered(1)),
            pl.BlockSpec((slabs, D_MODEL), const, pipeline_mode=pl.Buffered(1)),
            pl.BlockSpec((ATTN_WIDTH, D_MODEL), const, pipeline_mode=pl.Buffered(1)),
            pl.BlockSpec((D_MODEL, LANES), const),
            pl.BlockSpec((1, LANES), const),
            pl.BlockSpec((slabs, LANES), const),
            pl.BlockSpec((LANES, slabs), const),
            pl.BlockSpec((POOL_WIDTH, POOL_WIDTH), const),
            pl.BlockSpec((1, POOL_WIDTH), const),
            pl.BlockSpec((CONV_HALO, CONV_CH), const),
            pl.BlockSpec((1, CONV_CH), const),
            pl.BlockSpec((1, CONV_CH), const),
            pl.BlockSpec((1, CONV_CH), const),
        ],
        out_specs=(
            pl.BlockSpec((1, tm, POOL_WIDTH), row),
            pl.BlockSpec((1, 1, slabs, ATT_TQ), lambda bi, i: (bi, i, 0, 0)),
            pl.BlockSpec((1, tm, slabs), row),
            pl.BlockSpec((1, tm // ATT_KB, ATTN_WIDTH, ATT_KB), lambda bi, i: (bi, i, 0, 0)),
            pl.BlockSpec((1, tm, CONV_CH), row),
        ),
        out_shape=out_shape,
        scratch_shapes=[
            pltpu.VMEM((POOL_HALO + tm, POOL_WIDTH), F32),
            pltpu.VMEM((SUBLANES, CONV_HALO + tm, CONV_CH), F32),
            pltpu.VMEM((SCAN_HALO + tm, LANES), F32),
            pltpu.VMEM((1, LANES), F32),
        ],
        compiler_params=pltpu.CompilerParams(
            dimension_semantics=("arbitrary", "arbitrary"), vmem_limit_bytes=VMEM_LIMIT_BYTES),
        name="mixer_in",
    )(x3d, g, wm, wqt, wvt, wzf, fb, pq, pk, pw, ps, cw, cb, lg, lb)


_STREAMS = tuple((hh, qs) for hh in range(ATT_HG) for qs in range(ATT_TQ // ATT_QS))


def _attn_kernel(qt_ref, k_ref, vt_ref, o_ref, *scratch):
    i = pl.program_id(2)
    kb, ck, qs_w = ATT_KB, ATT_CK, ATT_QS
    n_chunks = kb // ck
    s_bufs = scratch[0:RING]
    p_bufs = scratch[RING:2 * RING]
    acc_buf = scratch[2 * RING]

    def kv_block(q_blk, n):
        return jnp.where(n == 0, q_blk, n - 1)

    def stage_a(n, par, sts=None):
        ahead = n > i
        q_blk = jnp.where(ahead, jnp.minimum(i + 1, pl.num_programs(2) - 1), i)
        row0 = pl.multiple_of(kv_block(q_blk, jnp.where(ahead, n - (i + 1), n)) * kb, kb)
        for st in (range(len(_STREAMS)) if sts is None else sts):
            hh, qs = _STREAMS[st]
            k_blk = k_ref[0, pl.ds(row0, kb), hh * SLAB:(hh + 1) * SLAB]
            qt = qt_ref[0, q_blk, hh * SLAB:(hh + 1) * SLAB, qs * qs_w:(qs + 1) * qs_w]
            s_bufs[par][st] = _dot(k_blk, qt)

    def stage_b(par, ms, masked, sts=None):
        new_ms, alphas = [], []
        for st in (range(len(_STREAMS)) if sts is None else sts):
            hh, qs = _STREAMS[st]

            def chunk(c, st=st, qs=qs):
                sc = s_bufs[par][st, c * ck:(c + 1) * ck, :]
                if masked and (c + 1) * ck - 1 > qs * qs_w:
                    kv_pos = c * ck + lax.broadcasted_iota(jnp.int32, (ck, qs_w), 0)
                    q_pos = qs * qs_w + lax.broadcasted_iota(jnp.int32, (ck, qs_w), 1)
                    sc = jnp.where(kv_pos <= q_pos, sc, NEG_BIG)
                return sc

            bmax = jnp.max(chunk(0), axis=0, keepdims=True)
            for c in range(1, n_chunks):
                bmax = jnp.maximum(bmax, jnp.max(chunk(c), axis=0, keepdims=True))
            m_new = jnp.maximum(ms[st], bmax)
            alphas.append(jnp.exp2(ms[st] - m_new))
            new_ms.append(m_new)
            for c in range(n_chunks):
                p_bufs[par][st, c * ck:(c + 1) * ck, :] = jnp.exp2(chunk(c) - m_new).astype(BF16)
        return tuple(new_ms), tuple(alphas)

    ones_rows = jnp.ones((ATT_VROWS - HEAD_DIM, kb), BF16)

    def stage_c(n, par, alphas, sts=None):
        jb = kv_block(i, n)
        for st in (range(len(_STREAMS)) if sts is None else sts):
            hh, qs = _STREAMS[st]
            v_blk = vt_ref[0, jb, hh * HEAD_DIM:(hh + 1) * HEAD_DIM, :]
            v_ext = jnp.concatenate([v_blk, ones_rows], axis=0)
            acc_buf[st] = alphas[st] * acc_buf[st] + _dot(v_ext, p_bufs[par][st])

    def tick(n, slot, ms, al1, al2, with_c=True):
        new_ms, new_al = [], []
        for st in range(len(_STREAMS)):
            stage_a(n + 2, (slot + 2) % RING, (st,))
            if with_c:
                stage_c(n - 2, (slot + 1) % RING, al2, (st,))
            m_st, a_st = stage_b(slot, ms, False, (st,))
            new_ms += m_st
            new_al += a_st
        return tuple(new_ms), tuple(new_al), al1

    def run_q_block(off):
        slot_of = lambda n: (n + off) % RING
        for st in range(len(_STREAMS)):
            acc_buf[st] = jnp.zeros((ATT_VROWS, qs_w), F32)
        ms = tuple(jnp.full((1, qs_w), NEG_BIG, F32) for _ in _STREAMS)

        if off == 0:
            @pl.when(i == 0)
            def _():
                stage_a(0, slot_of(0))
                stage_a(1, slot_of(1))

        ms, al1 = stage_b(slot_of(0), ms, True)
        stage_a(2, slot_of(2))
        state = (ms, al1, al1)
        state = lax.cond(i >= 1, lambda _: tick(1, slot_of(1), *state, with_c=False), lambda _: state, 0)

        def tick_ring(k, carry):
            n = RING * k + 2
            for r in range(RING):
                carry = tick(n + r, slot_of(2 + r), *carry)
            return carry

        n_ring = jnp.maximum(i - 1, 0)
        state = lax.fori_loop(0, n_ring // RING, tick_ring, state)
        left = lax.rem(n_ring, RING)

        state = lax.cond(left >= 1, lambda _: tick(i - left + 1, slot_of(2), *state), lambda _: state, 0)
        state = lax.cond(left == 2, lambda _: tick(i, slot_of(0), *state), lambda _: state, 0)
        _, al1, al2 = state
        for slot in range(RING):
            pl.when(jnp.logical_and(i >= 1, lax.rem(i + off + RING - 1, RING) == slot))(
                functools.partial(stage_c, i - 1, slot, al2))
        for slot in range(RING):
            pl.when(lax.rem(i + off, RING) == slot)(functools.partial(stage_c, i, slot, al1))

    off_is_one = lax.rem(i, RING) == 1
    pl.when(off_is_one)(functools.partial(run_q_block, 1))
    pl.when(jnp.logical_not(off_is_one))(functools.partial(run_q_block, 0))
    for st, (hh, qs) in enumerate(_STREAMS):
        acc = acc_buf[st]
        out = acc[0:HEAD_DIM, :] / acc[HEAD_DIM:HEAD_DIM + 1, :]
        o_ref[0, hh * HEAD_DIM:(hh + 1) * HEAD_DIM, qs * qs_w:(qs + 1) * qs_w] = out.astype(BF16)


def _attn_call(qt, ks, vt):
    b, s, _ = ks.shape
    tq = ATT_TQ
    n_groups = ATTN_HEADS // ATT_HG
    n_streams = len(_STREAMS)
    return pl.pallas_call(
        _attn_kernel,
        grid=(b, n_groups, s // tq),
        in_specs=[
            pl.BlockSpec((1, s // tq, ATT_HG * SLAB, tq), lambda bi, hg, i: (bi, 0, hg, 0)),
            pl.BlockSpec((1, s, ATT_HG * SLAB), lambda bi, hg, i: (bi, 0, hg)),
            pl.BlockSpec((1, s // ATT_KB, ATT_HG * HEAD_DIM, ATT_KB), lambda bi, hg, i: (bi, 0, hg, 0)),
        ],
        out_specs=pl.BlockSpec((1, ATT_HG * HEAD_DIM, tq), lambda bi, hg, i: (bi, hg, i)),
        out_shape=jax.ShapeDtypeStruct((b, ATTN_WIDTH, s), BF16),
        scratch_shapes=(
            [pltpu.VMEM((n_streams, ATT_KB, ATT_QS), F32)] * RING
            + [pltpu.VMEM((n_streams, ATT_KB, ATT_QS), BF16)] * RING
            + [pltpu.VMEM((n_streams, ATT_VROWS, ATT_QS), F32)]),
        compiler_params=pltpu.CompilerParams(
            dimension_semantics=("arbitrary", "arbitrary", "arbitrary"), vmem_limit_bytes=VMEM_LIMIT_BYTES),
        name="attn",
    )(qt, ks, vt)


def _block_diag(pool_w):
    g, c, d = pool_w.shape
    out = jnp.zeros((g * c, g * d), pool_w.dtype)
    for gi in range(g):
        out = lax.dynamic_update_slice(out, pool_w[gi], (gi * c, gi * d))
    return out


def _head_slabs(w):
    dm = w.shape[0]
    w3 = w.reshape(dm, ATTN_HEADS, HEAD_DIM)
    return jnp.pad(w3, ((0, 0), (0, 0), (0, SLAB - HEAD_DIM))).reshape(dm, ATTN_HEADS * SLAB)


def _placement():
    pq = np.zeros((LANES, ATTN_HEADS * SLAB), np.float32)
    pk = np.zeros((LANES, ATTN_HEADS * SLAB), np.float32)
    for h in range(ATTN_HEADS):
        base = h * SLAB + HEAD_DIM
        for part in range(3):
            pq[part * ATTN_HEADS + h, base + part] = 1.0
            pq[FS_ONE, base + 3 + part] = 1.0
            pk[FS_ONE, base + part] = 1.0
            pk[part * ATTN_HEADS + h, base + 3 + part] = -1.0
    return jnp.asarray(pq.T, BF16), jnp.asarray(pk, BF16)


def _rep3(a):
    a3 = jnp.concatenate([a, a, a], axis=-1)
    return jnp.pad(a3, ((0, 0), (0, LANES - a3.shape[-1])))


def kernel(x, ffn1_norm, ffn1_w_gate, ffn1_w_up, ffn1_w_down, mix_norm, w_in, pool_w, pool_scale, forget_bias, conv_w, conv_b, conv_ln_g, conv_ln_b, w_out, ffn2_norm, ffn2_w_gate, ffn2_w_up, ffn2_w_down, final_norm):
    b, s, d = x.shape
    depth = w_in.shape[0]
    t = b * s
    x2d = x.reshape(t, d)
    fin = final_norm.reshape(1, d)
    c_q = POOL_WIDTH
    c_k = c_q + ATTN_WIDTH
    c_v = c_k + ATTN_WIDTH
    c_zf = c_v + ATTN_WIDTH
    c_glu = c_zf + ATTN_HEADS
    pq_t, pk = _placement()
    for l in range(depth):
        x2d = _ffn_call(x2d, ffn1_norm[l].reshape(1, d), ffn1_w_gate[l].astype(BF16), ffn1_w_up[l].astype(BF16),
                        ffn1_w_down[l].astype(BF16))
        wl = w_in[l]
        wm = jnp.concatenate([wl[:, :c_q], wl[:, c_glu:], _head_slabs(wl[:, c_k:c_v])], axis=1).astype(BF16)
        wqt = _head_slabs(wl[:, c_q:c_k]).T.astype(BF16)
        wvt = wl[:, c_v:c_zf].T.astype(BF16)
        wzf = _rep3(wl[:, c_zf:c_glu]).astype(BF16)
        fb = _rep3(forget_bias[l].reshape(1, ATTN_HEADS))
        pw = _block_diag(pool_w[l]).astype(BF16)
        cw = jnp.pad(conv_w[l], ((0, CONV_HALO - CONV_KERNEL), (0, 0)))
        ya, qt, ks, vt, yc = _mixer_in_call(
            x2d.reshape(b, s, d), mix_norm[l].reshape(1, d), wm, wqt, wvt, wzf, fb, pq_t, pk, pw,
            pool_scale[l].reshape(1, POOL_WIDTH), cw, conv_b[l].reshape(1, CONV_CH),
            conv_ln_g[l].reshape(1, CONV_CH), conv_ln_b[l].reshape(1, CONV_CH))
        ybt = _attn_call(qt, ks, vt)
        x2d = _mix_ffn_call(x2d.reshape(b, s, d), ya, ybt, yc, w_out[l].astype(BF16),
                            ffn2_norm[l].reshape(1, d), ffn2_w_gate[l].astype(BF16), ffn2_w_up[l].astype(BF16),
                            ffn2_w_down[l].astype(BF16), fin, l == depth - 1).reshape(t, d)
    return x2d.reshape(b, s, d)
```

```python
import functools
import math

import numpy as np
import jax
import jax.numpy as jnp
from jax import lax
from jax.experimental import pallas as pl
from jax.experimental.pallas import tpu as pltpu

D_MODEL = 1024
D_FF = 2816
NORM_EPS = 1e-6
POOL_WINDOWS = (2, 4, 8, 16)
POOL_GROUP_DIM = 64
POOL_WIDTH = 256
ATTN_HEADS = 8
HEAD_DIM = 64
ATTN_WIDTH = 512
CONV_CH = 256
CONV_KERNEL = 31

LANES = 128
SUBLANES = 8
VMEM_LIMIT_BYTES = 56 * 1024 * 1024

FFN_TM = 512
FFN_FC = 256
MIX_TM = 512
POOL_HALO = 16
CONV_HALO = 32
SCAN_HALO = MIX_TM // 2
CONV_ROWS = 64
SLAB = LANES
ATT_TQ = 512
ATT_QS = 256
ATT_CK = 128
ATT_KB = 512
ATT_HG = 2
RING = 3
NEG_BIG = -1e30
LOG2E = math.log2(math.e)
ATT_VROWS = HEAD_DIM + 16

FS_ONE = 3 * ATTN_HEADS

BF16 = jnp.bfloat16
F32 = jnp.float32
NT_DIMS = (((1,), (1,)), ((), ()))
TN_DIMS = (((0,), (0,)), ((), ()))


def _rms_norm(x, g):
    ms = jnp.mean(x * x, axis=-1, keepdims=True)
    return x * lax.rsqrt(ms + NORM_EPS) * g


def _silu(x):
    return x * jax.nn.sigmoid(x)


def _dot(a, b):
    return jnp.dot(a, b, preferred_element_type=F32)


def _swiglu_residual(x, g_ref, wg_ref, wu_ref, wd_ref):
    hn = _rms_norm(x, g_ref[...]).astype(BF16)
    acc = jnp.zeros(x.shape, F32)
    for c in range(D_FF // FFN_FC):
        lo = c * FFN_FC
        gate = _dot(hn, wg_ref[:, lo:lo + FFN_FC])
        up = _dot(hn, wu_ref[:, lo:lo + FFN_FC])
        act = (_silu(gate) * up).astype(BF16)
        acc = acc + _dot(act, wd_ref[lo:lo + FFN_FC, :])
    return x + 0.5 * acc


def _ffn_kernel(x_ref, g_ref, wg_ref, wu_ref, wd_ref, o_ref):
    o_ref[...] = _swiglu_residual(x_ref[...], g_ref, wg_ref, wu_ref, wd_ref)


def _mix_ffn_kernel(x_ref, ya_ref, ybt_ref, yc_ref, wo_ref, g_ref, wg_ref, wu_ref, wd_ref, fin_ref, o_ref,
                    *, final_norm):
    a_end = POOL_WIDTH
    b_end = POOL_WIDTH + ATTN_WIDTH
    y = _dot(ya_ref[0], wo_ref[0:a_end, :])
    y = y + lax.dot_general(ybt_ref[0], wo_ref[a_end:b_end, :], TN_DIMS, preferred_element_type=F32)
    y = y + _dot(yc_ref[0], wo_ref[b_end:D_MODEL, :])
    out = _swiglu_residual(x_ref[0] + y, g_ref, wg_ref, wu_ref, wd_ref)
    if final_norm:
        out = _rms_norm(out, fin_ref[...])
    o_ref[0] = out


def _ffn_call(x2d, g, wg, wu, wd, layer):
    t = x2d.shape[0]
    const = lambda i: (0, 0)
    of_layer = lambda i: (layer, 0, 0)
    return pl.pallas_call(
        _ffn_kernel,
        grid=(t // FFN_TM,),
        in_specs=[
            pl.BlockSpec((FFN_TM, D_MODEL), lambda i: (i, 0)),
            pl.BlockSpec((1, D_MODEL), const),
            pl.BlockSpec((None, D_MODEL, D_FF), of_layer, pipeline_mode=pl.Buffered(1)),
            pl.BlockSpec((None, D_MODEL, D_FF), of_layer, pipeline_mode=pl.Buffered(1)),
            pl.BlockSpec((None, D_FF, D_MODEL), of_layer, pipeline_mode=pl.Buffered(1)),
        ],
        out_specs=pl.BlockSpec((FFN_TM, D_MODEL), lambda i: (i, 0)),
        out_shape=jax.ShapeDtypeStruct(x2d.shape, F32),
        compiler_params=pltpu.CompilerParams(
            dimension_semantics=("arbitrary",), vmem_limit_bytes=VMEM_LIMIT_BYTES),
        name="ffn",
    )(x2d, g, wg, wu, wd)


def _mix_ffn_call(x3d, ya, ybt, yc, wo, g, wg, wu, wd, fin, final_norm, layer):
    b, s, _ = x3d.shape
    tm = FFN_TM
    row = lambda bi, i: (bi, i, 0)
    const = lambda bi, i: (0, 0)
    of_layer = lambda bi, i: (layer, 0, 0)
    return pl.pallas_call(
        functools.partial(_mix_ffn_kernel, final_norm=final_norm),
        grid=(b, s // tm),
        in_specs=[
            pl.BlockSpec((1, tm, D_MODEL), row),
            pl.BlockSpec((1, tm, POOL_WIDTH), row),
            pl.BlockSpec((1, ATTN_WIDTH, tm), lambda bi, i: (bi, 0, i)),
            pl.BlockSpec((1, tm, CONV_CH), row),
            pl.BlockSpec((None, D_MODEL, D_MODEL), of_layer, pipeline_mode=pl.Buffered(1)),
            pl.BlockSpec((1, D_MODEL), const),
            pl.BlockSpec((None, D_MODEL, D_FF), of_layer, pipeline_mode=pl.Buffered(1)),
            pl.BlockSpec((None, D_MODEL, D_FF), of_layer, pipeline_mode=pl.Buffered(1)),
            pl.BlockSpec((None, D_FF, D_MODEL), of_layer, pipeline_mode=pl.Buffered(1)),
            pl.BlockSpec((1, D_MODEL), const),
        ],
        out_specs=pl.BlockSpec((1, tm, D_MODEL), row),
        out_shape=jax.ShapeDtypeStruct(x3d.shape, F32),
        compiler_params=pltpu.CompilerParams(
            dimension_semantics=("arbitrary", "arbitrary"), vmem_limit_bytes=VMEM_LIMIT_BYTES),
        name="mix_ffn",
    )(x3d, ya, ybt, yc, wo, g, wg, wu, wd, fin)


_C_POOL = 0
_C_GA = 256
_C_GG = 512
_C_K = 768
_C_END = _C_K + ATTN_HEADS * SLAB


def _shift_rows(buf_ref, rows, d):
    halo = buf_ref.shape[0] - rows
    return buf_ref[halo - d:halo - d + rows, :]


def _mixer_in_kernel(x_ref, g_ref, wm_ref, wqt_ref, wvt_ref, wzf_ref, fb_ref, pq_ref, pk_ref,
                     pw_ref, ps_ref, cw_ref, cb_ref, lg_ref, lb_ref,
                     ya_ref, qt_ref, k_ref, vt_ref, yc_ref,
                     pbuf, cbuf, sbuf, fcarry):
    i = pl.program_id(1)
    tm = MIX_TM

    @pl.when(i == 0)
    def _():
        pbuf[0:POOL_HALO, :] = jnp.zeros((POOL_HALO, POOL_WIDTH), F32)
        cbuf[0, 0:CONV_HALO, :] = jnp.zeros((CONV_HALO, CONV_CH), F32)
        sbuf[0:SCAN_HALO, :] = jnp.zeros((SCAN_HALO, LANES), F32)
        fcarry[...] = jnp.zeros((1, LANES), F32)

    hn = _rms_norm(x_ref[0], g_ref[...]).astype(BF16)

    z = _dot(hn, wzf_ref[...]) + fb_ref[...]
    ga = _dot(hn, wm_ref[:, _C_GA:_C_GG])
    gg = _dot(hn, wm_ref[:, _C_GG:_C_K])
    u = _dot(hn, wm_ref[:, _C_POOL:_C_GA])

    logf = jnp.minimum(z, 0.0) - jnp.log1p(jnp.exp(-jnp.abs(z)))
    sbuf[SCAN_HALO:SCAN_HALO + tm, :] = logf
    d = 1
    cur = logf
    while d < tm:
        cur = sbuf[SCAN_HALO:SCAN_HALO + tm, :] + _shift_rows(sbuf, tm, d)
        sbuf[SCAN_HALO:SCAN_HALO + tm, :] = cur
        d *= 2
    ftile = cur + fcarry[...]
    fcarry[...] = ftile[tm - 1:tm, :]

    f2 = ftile * LOG2E
    hi = f2.astype(BF16).astype(F32)
    r1 = f2 - hi
    mid = r1.astype(BF16).astype(F32)
    lo = (r1 - mid).astype(BF16).astype(F32)
    lane = lax.broadcasted_iota(jnp.int32, (1, LANES), 1)
    fs = jnp.where(lane < ATTN_HEADS, hi,
                   jnp.where(lane < 2 * ATTN_HEADS, mid,
                             jnp.where(lane < FS_ONE, lo,
                                       jnp.where(lane == FS_ONE, 1.0, 0.0)))).astype(BF16)

    rows = POOL_HALO + tm
    pbuf[POOL_HALO:rows, :] = u
    lane_p = lax.broadcasted_iota(jnp.int32, (1, POOL_WIDTH), 1)
    group = lane_p // POOL_GROUP_DIM
    ext = pbuf[...]
    sel = jnp.zeros((tm, POOL_WIDTH), F32)
    span = 1
    for gi, w in enumerate(POOL_WINDOWS):
        while span < w:
            ext = ext + pltpu.roll(ext, span, 0)
            span *= 2
        sel = jnp.where(group == gi, ext[POOL_HALO:, :], sel)
    win = jnp.zeros((1, POOL_WIDTH), F32)
    for gi, w in enumerate(POOL_WINDOWS):
        win = jnp.where(group == gi, float(w), win)
    pos = (i * tm + lax.broadcasted_iota(jnp.int32, (tm, 1), 0) + 1).astype(F32)
    count = jnp.minimum(pos, win)
    pooled = (sel / count - u).astype(BF16)
    pbuf[0:POOL_HALO, :] = pbuf[tm:tm + POOL_HALO, :]

    crow = CONV_HALO + tm
    cbuf[0, CONV_HALO:crow, :] = ga * jax.nn.sigmoid(gg)
    base = cbuf[0]
    for r in range(1, SUBLANES):
        cbuf[r] = pltpu.roll(base, r, 0)
    cb = cb_ref[...]
    lg = lg_ref[...]
    lb = lb_ref[...]
    for rc in range(tm // CONV_ROWS):
        acc = jnp.broadcast_to(cb, (CONV_ROWS, CONV_CH))
        for j in range(CONV_KERNEL):
            delay = CONV_KERNEL - 1 - j
            r, a = delay % SUBLANES, delay // SUBLANES
            lo_row = CONV_HALO + rc * CONV_ROWS - a * SUBLANES
            acc = acc + cw_ref[j:j + 1, :] * cbuf[r, lo_row:lo_row + CONV_ROWS, :]
        mu = jnp.mean(acc, axis=-1, keepdims=True)
        xc = acc - mu
        var = jnp.mean(xc * xc, axis=-1, keepdims=True)
        yn = xc * lax.rsqrt(var + NORM_EPS) * lg + lb
        yc_ref[0, rc * CONV_ROWS:(rc + 1) * CONV_ROWS, :] = _silu(yn).astype(BF16)
    cbuf[0, 0:CONV_HALO, :] = cbuf[0, tm:tm + CONV_HALO, :]

    qt = lax.dot_general(wqt_ref[...], hn, NT_DIMS, preferred_element_type=F32) * (LOG2E / math.sqrt(HEAD_DIM))
    ks = _dot(hn, wm_ref[:, _C_K:_C_END])
    vt = lax.dot_general(wvt_ref[...], hn, NT_DIMS, preferred_element_type=F32).astype(BF16)
    for c in range(tm // ATT_KB):
        vt_ref[0, c] = vt[:, c * ATT_KB:(c + 1) * ATT_KB]
    qt = qt + lax.dot_general(pq_ref[...], fs, NT_DIMS, preferred_element_type=F32)
    qt_ref[0, 0] = qt.astype(BF16)
    k_ref[0] = (ks + _dot(fs, pk_ref[...])).astype(BF16)
    ya_ref[0] = (_dot(pooled, pw_ref[...]) * ps_ref[...]).astype(BF16)


def _mixer_in_call(x3d, g, wm, wqt, wvt, wzf, fb, pq, pk, pw, ps, cw, cb, lg, lb):
    b, s, _ = x3d.shape
    tm = MIX_TM
    assert tm == ATT_TQ == ATT_KB
    const = lambda bi, i: (0, 0)
    row = lambda bi, i: (bi, i, 0)
    slabs = ATTN_HEADS * SLAB
    out_shape = (
        jax.ShapeDtypeStruct((b, s, POOL_WIDTH), BF16),
        jax.ShapeDtypeStruct((b, s // ATT_TQ, slabs, ATT_TQ), BF16),
        jax.ShapeDtypeStruct((b, s, slabs), BF16),
        jax.ShapeDtypeStruct((b, s // ATT_KB, ATTN_WIDTH, ATT_KB), BF16),
        jax.ShapeDtypeStruct((b, s, CONV_CH), BF16),
    )
    return pl.pallas_call(
        _mixer_in_kernel,
        grid=(b, s // tm),
        in_specs=[
            pl.BlockSpec((1, tm, D_MODEL), row),
            pl.BlockSpec((1, D_MODEL), const),
            pl.BlockSpec((D_MODEL, _C_END), const, pipeline_mode=pl.Buffered(1)),
            pl.BlockSpec((slabs, D_MODEL), const, pipeline_mode=pl.Buffered(1)),
            pl.BlockSpec((ATTN_WIDTH, D_MODEL), const, pipeline_mode=pl.Buffered(1)),
            pl.BlockSpec((D_MODEL, LANES), const),
            pl.BlockSpec((1, LANES), const),
            pl.BlockSpec((slabs, LANES), const),
            pl.BlockSpec((LANES, slabs), const),
            pl.BlockSpec((POOL_WIDTH, POOL_WIDTH), const),
            pl.BlockSpec((1, POOL_WIDTH), const),
            pl.BlockSpec((CONV_HALO, CONV_CH), const),
            pl.BlockSpec((1, CONV_CH), const),
            pl.BlockSpec((1, CONV_CH), const),
            pl.BlockSpec((1, CONV_CH), const),
        ],
        out_specs=(
            pl.BlockSpec((1, tm, POOL_WIDTH), row),
            pl.BlockSpec((1, 1, slabs, ATT_TQ), lambda bi, i: (bi, i, 0, 0)),
            pl.BlockSpec((1, tm, slabs), row),
            pl.BlockSpec((1, tm // ATT_KB, ATTN_WIDTH, ATT_KB), lambda bi, i: (bi, i, 0, 0)),
            pl.BlockSpec((1, tm, CONV_CH), row),
        ),
        out_shape=out_shape,
        scratch_shapes=[
            pltpu.VMEM((POOL_HALO + tm, POOL_WIDTH), F32),
            pltpu.VMEM((SUBLANES, CONV_HALO + tm, CONV_CH), F32),
            pltpu.VMEM((SCAN_HALO + tm, LANES), F32),
            pltpu.VMEM((1, LANES), F32),
        ],
        compiler_params=pltpu.CompilerParams(
            dimension_semantics=("arbitrary", "arbitrary"), vmem_limit_bytes=VMEM_LIMIT_BYTES),
        name="mixer_in",
    )(x3d, g, wm, wqt, wvt, wzf, fb, pq, pk, pw, ps, cw, cb, lg, lb)


_STREAMS = tuple((hh, qs) for hh in range(ATT_HG) for qs in range(ATT_TQ // ATT_QS))


def _attn_kernel(qt_ref, k_ref, vt_ref, o_ref, *scratch):
    i = pl.program_id(2)
    kb, ck, qs_w = ATT_KB, ATT_CK, ATT_QS
    n_chunks = kb // ck
    s_bufs = scratch[0:RING]
    p_bufs = scratch[RING:2 * RING]
    acc_buf = scratch[2 * RING]

    def kv_block(q_blk, n):
        return jnp.where(n == 0, q_blk, n - 1)

    def stage_a(n, par, sts=None, rows=(0, ATT_KB)):
        ahead = n > i
        q_blk = jnp.where(ahead, jnp.minimum(i + 1, pl.num_programs(2) - 1), i)
        row0 = pl.multiple_of(kv_block(q_blk, jnp.where(ahead, n - (i + 1), n)) * kb, kb)
        lo, hi = rows
        for st in (range(len(_STREAMS)) if sts is None else sts):
            hh, qs = _STREAMS[st]
            k_blk = k_ref[0, pl.ds(row0 + lo, hi - lo), hh * SLAB:(hh + 1) * SLAB]
            qt = qt_ref[0, q_blk, hh * SLAB:(hh + 1) * SLAB, qs * qs_w:(qs + 1) * qs_w]
            s_bufs[par][st, lo:hi, :] = _dot(k_blk, qt)

    def chunk(par, st, c, masked):
        sc = s_bufs[par][st, c * ck:(c + 1) * ck, :]
        qs = _STREAMS[st][1]
        if masked and (c + 1) * ck - 1 > qs * qs_w:
            kv_pos = c * ck + lax.broadcasted_iota(jnp.int32, (ck, qs_w), 0)
            q_pos = qs * qs_w + lax.broadcasted_iota(jnp.int32, (ck, qs_w), 1)
            sc = jnp.where(kv_pos <= q_pos, sc, NEG_BIG)
        return sc

    def block_max(par, st, masked):
        bmax = jnp.max(chunk(par, st, 0, masked), axis=0, keepdims=True)
        for c in range(1, n_chunks):
            bmax = jnp.maximum(bmax, jnp.max(chunk(par, st, c, masked), axis=0, keepdims=True))
        return bmax

    def numerators(par, st, m_new, masked):
        for c in range(n_chunks):
            p_bufs[par][st, c * ck:(c + 1) * ck, :] = jnp.exp2(chunk(par, st, c, masked) - m_new).astype(BF16)

    def stage_b(par, ms, masked, sts=None):
        new_ms, alphas = [], []
        for st in (range(len(_STREAMS)) if sts is None else sts):
            m_new = jnp.maximum(ms[st], block_max(par, st, masked))
            alphas.append(jnp.exp2(ms[st] - m_new))
            new_ms.append(m_new)
            numerators(par, st, m_new, masked)
        return tuple(new_ms), tuple(alphas)

    ones_rows = jnp.ones((ATT_VROWS - HEAD_DIM, kb), BF16)

    def stage_c(n, par, alphas, sts=None):
        jb = kv_block(i, n)
        for st in (range(len(_STREAMS)) if sts is None else sts):
            hh, qs = _STREAMS[st]
            v_blk = vt_ref[0, jb, hh * HEAD_DIM:(hh + 1) * HEAD_DIM, :]
            v_ext = jnp.concatenate([v_blk, ones_rows], axis=0)
            acc_buf[st] = alphas[st] * acc_buf[st] + _dot(v_ext, p_bufs[par][st])

    def tick(n, slot, ms, al1, al2, with_c=True):
        new_ms, new_al = [], []
        for st in range(len(_STREAMS)):
            stage_a(n + 2, (slot + 2) % RING, (st,), (0, kb // 2))
            m_new = jnp.maximum(ms[st], block_max(slot, st, False))
            stage_a(n + 2, (slot + 2) % RING, (st,), (kb // 2, kb))
            if with_c:
                stage_c(n - 2, (slot + 1) % RING, al2, (st,))
            numerators(slot, st, m_new, False)
            new_ms.append(m_new)
            new_al.append(jnp.exp2(ms[st] - m_new))
        return tuple(new_ms), tuple(new_al), al1

    def run_q_block(off):
        slot_of = lambda n: (n + off) % RING
        for st in range(len(_STREAMS)):
            acc_buf[st] = jnp.zeros((ATT_VROWS, qs_w), F32)
        ms = tuple(jnp.full((1, qs_w), NEG_BIG, F32) for _ in _STREAMS)

        if off == 0:
            @pl.when(i == 0)
            def _():
                stage_a(0, slot_of(0))
                stage_a(1, slot_of(1))

        ms, al1 = stage_b(slot_of(0), ms, True)
        stage_a(2, slot_of(2))
        state = (ms, al1, al1)
        state = lax.cond(i >= 1, lambda _: tick(1, slot_of(1), *state, with_c=False), lambda _: state, 0)

        def tick_ring(k, carry):
            n = RING * k + 2
            for r in range(RING):
                carry = tick(n + r, slot_of(2 + r), *carry)
            return carry

        n_ring = jnp.maximum(i - 1, 0)
        state = lax.fori_loop(0, n_ring // RING, tick_ring, state)
        left = lax.rem(n_ring, RING)

        state = lax.cond(left >= 1, lambda _: tick(i - left + 1, slot_of(2), *state), lambda _: state, 0)
        state = lax.cond(left == 2, lambda _: tick(i, slot_of(0), *state), lambda _: state, 0)
        _, al1, al2 = state
        for slot in range(RING):
            pl.when(jnp.logical_and(i >= 1, lax.rem(i + off + RING - 1, RING) == slot))(
                functools.partial(stage_c, i - 1, slot, al2))
        for slot in range(RING):
            pl.when(lax.rem(i + off, RING) == slot)(functools.partial(stage_c, i, slot, al1))

    off_is_one = lax.rem(i, RING) == 1
    pl.when(off_is_one)(functools.partial(run_q_block, 1))
    pl.when(jnp.logical_not(off_is_one))(functools.partial(run_q_block, 0))
    for st, (hh, qs) in enumerate(_STREAMS):
        acc = acc_buf[st]
        out = acc[0:HEAD_DIM, :] / acc[HEAD_DIM:HEAD_DIM + 1, :]
        o_ref[0, hh * HEAD_DIM:(hh + 1) * HEAD_DIM, qs * qs_w:(qs + 1) * qs_w] = out.astype(BF16)


def _attn_call(qt, ks, vt):
    b, s, _ = ks.shape
    tq = ATT_TQ
    n_groups = ATTN_HEADS // ATT_HG
    n_streams = len(_STREAMS)
    return pl.pallas_call(
        _attn_kernel,
        grid=(b, n_groups, s // tq),
        in_specs=[
            pl.BlockSpec((1, s // tq, ATT_HG * SLAB, tq), lambda bi, hg, i: (bi, 0, hg, 0)),
            pl.BlockSpec((1, s, ATT_HG * SLAB), lambda bi, hg, i: (bi, 0, hg)),
            pl.BlockSpec((1, s // ATT_KB, ATT_HG * HEAD_DIM, ATT_KB), lambda bi, hg, i: (bi, 0, hg, 0)),
        ],
        out_specs=pl.BlockSpec((1, ATT_HG * HEAD_DIM, tq), lambda bi, hg, i: (bi, hg, i)),
        out_shape=jax.ShapeDtypeStruct((b, ATTN_WIDTH, s), BF16),
        scratch_shapes=(
            [pltpu.VMEM((n_streams, ATT_KB, ATT_QS), F32)] * RING
            + [pltpu.VMEM((n_streams, ATT_KB, ATT_QS), BF16)] * RING
            + [pltpu.VMEM((n_streams, ATT_VROWS, ATT_QS), F32)]),
        compiler_params=pltpu.CompilerParams(
            dimension_semantics=("arbitrary", "arbitrary", "arbitrary"), vmem_limit_bytes=VMEM_LIMIT_BYTES),
        name="attn",
    )(qt, ks, vt)


def _block_diag(pool_w):
    g, c, d = pool_w.shape
    out = jnp.zeros((g * c, g * d), pool_w.dtype)
    for gi in range(g):
        out = lax.dynamic_update_slice(out, pool_w[gi], (gi * c, gi * d))
    return out


def _head_slabs(w):
    dm = w.shape[0]
    w3 = w.reshape(dm, ATTN_HEADS, HEAD_DIM)
    return jnp.pad(w3, ((0, 0), (0, 0), (0, SLAB - HEAD_DIM))).reshape(dm, ATTN_HEADS * SLAB)


def _placement():
    pq = np.zeros((LANES, ATTN_HEADS * SLAB), np.float32)
    pk = np.zeros((LANES, ATTN_HEADS * SLAB), np.float32)
    for h in range(ATTN_HEADS):
        base = h * SLAB + HEAD_DIM
        for part in range(3):
            pq[part * ATTN_HEADS + h, base + part] = 1.0
            pq[FS_ONE, base + 3 + part] = 1.0
            pk[FS_ONE, base + part] = 1.0
            pk[part * ATTN_HEADS + h, base + 3 + part] = -1.0
    return jnp.asarray(pq.T, BF16), jnp.asarray(pk, BF16)


def _rep3(a):
    a3 = jnp.concatenate([a, a, a], axis=-1)
    return jnp.pad(a3, ((0, 0), (0, LANES - a3.shape[-1])))


def kernel(x, ffn1_norm, ffn1_w_gate, ffn1_w_up, ffn1_w_down, mix_norm, w_in, pool_w, pool_scale, forget_bias, conv_w, conv_b, conv_ln_g, conv_ln_b, w_out, ffn2_norm, ffn2_w_gate, ffn2_w_up, ffn2_w_down, final_norm):
    b, s, d = x.shape
    depth = w_in.shape[0]
    t = b * s
    x2d = x.reshape(t, d)
    fin = final_norm.reshape(1, d)
    c_q = POOL_WIDTH
    c_k = c_q + ATTN_WIDTH
    c_v = c_k + ATTN_WIDTH
    c_zf = c_v + ATTN_WIDTH
    c_glu = c_zf + ATTN_HEADS
    pq_t, pk = _placement()
    ffn1 = tuple(w.astype(BF16) for w in (ffn1_w_gate, ffn1_w_up, ffn1_w_down))
    ffn2 = tuple(w.astype(BF16) for w in (ffn2_w_gate, ffn2_w_up, ffn2_w_down))
    wo = w_out.astype(BF16)
    for l in range(depth):
        x2d = _ffn_call(x2d, ffn1_norm[l].reshape(1, d), *ffn1, l)
        wl = w_in[l]
        wm = jnp.concatenate([wl[:, :c_q], wl[:, c_glu:], _head_slabs(wl[:, c_k:c_v])], axis=1).astype(BF16)
        wqt = _head_slabs(wl[:, c_q:c_k]).T.astype(BF16)
        wvt = wl[:, c_v:c_zf].T.astype(BF16)
        wzf = _rep3(wl[:, c_zf:c_glu]).astype(BF16)
        fb = _rep3(forget_bias[l].reshape(1, ATTN_HEADS))
        pw = _block_diag(pool_w[l]).astype(BF16)
        cw = jnp.pad(conv_w[l], ((0, CONV_HALO - CONV_KERNEL), (0, 0)))
        ya, qt, ks, vt, yc = _mixer_in_call(
            x2d.reshape(b, s, d), mix_norm[l].reshape(1, d), wm, wqt, wvt, wzf, fb, pq_t, pk, pw,
            pool_scale[l].reshape(1, POOL_WIDTH), cw, conv_b[l].reshape(1, CONV_CH),
            conv_ln_g[l].reshape(1, CONV_CH), conv_ln_b[l].reshape(1, CONV_CH))
        ybt = _attn_call(qt, ks, vt)
        x2d = _mix_ffn_call(x2d.reshape(b, s, d), ya, ybt, yc, wo, ffn2_norm[l].reshape(1, d), *ffn2,
                            fin, l == depth - 1, l).reshape(t, d)
    return x2d.reshape(b, s, d)
```

```python
import functools
import math

import numpy as np
import jax
import jax.numpy as jnp
from jax import lax
from jax.experimental import pallas as pl
from jax.experimental.pallas import tpu as pltpu

D_MODEL = 1024
D_FF = 2816
NORM_EPS = 1e-6
POOL_WINDOWS = (2, 4, 8, 16)
POOL_GROUP_DIM = 64
POOL_WIDTH = 256
ATTN_HEADS = 8
HEAD_DIM = 64
ATTN_WIDTH = 512
CONV_CH = 256
CONV_KERNEL = 31

LANES = 128
SUBLANES = 8
VMEM_LIMIT_BYTES = 56 * 1024 * 1024

FFN_TM = 512
FFN_FC = 256
MIX_TM = 512
POOL_HALO = 16
CONV_HALO = 32
SCAN_HALO = MIX_TM // 2
CONV_ROWS = 64
SLAB = LANES
ATT_TQ = 512
ATT_QS = 256
ATT_CK = 128
ATT_KB = 512
ATT_HG = 2
RING = 3
NEG_BIG = -1e30
LOG2E = math.log2(math.e)
ATT_VROWS = HEAD_DIM + 16

FS_ONE = 3 * ATTN_HEADS

BF16 = jnp.bfloat16
F32 = jnp.float32
NT_DIMS = (((1,), (1,)), ((), ()))
TN_DIMS = (((0,), (0,)), ((), ()))


def _rms_norm(x, g):
    ms = jnp.mean(x * x, axis=-1, keepdims=True)
    return x * lax.rsqrt(ms + NORM_EPS) * g


def _silu(x):
    return x * jax.nn.sigmoid(x)


def _dot(a, b):
    return jnp.dot(a, b, preferred_element_type=F32)


def _swiglu_residual(x, g_ref, wg_ref, wu_ref, wd_ref):
    hn = _rms_norm(x, g_ref[...]).astype(BF16)
    acc = jnp.zeros(x.shape, F32)
    for c in range(D_FF // FFN_FC):
        lo = c * FFN_FC
        gate = _dot(hn, wg_ref[:, lo:lo + FFN_FC])
        up = _dot(hn, wu_ref[:, lo:lo + FFN_FC])
        act = (_silu(gate) * up).astype(BF16)
        acc = acc + _dot(act, wd_ref[lo:lo + FFN_FC, :])
    return x + 0.5 * acc


def _ffn_kernel(x_ref, g_ref, wg_ref, wu_ref, wd_ref, o_ref):
    o_ref[...] = _swiglu_residual(x_ref[...], g_ref, wg_ref, wu_ref, wd_ref)


def _mix_ffn_kernel(x_ref, ya_ref, ybt_ref, yc_ref, wo_ref, g_ref, wg_ref, wu_ref, wd_ref, fin_ref, o_ref,
                    *, final_norm):
    a_end = POOL_WIDTH
    b_end = POOL_WIDTH + ATTN_WIDTH
    y = _dot(ya_ref[0], wo_ref[0:a_end, :])
    y = y + lax.dot_general(ybt_ref[0], wo_ref[a_end:b_end, :], TN_DIMS, preferred_element_type=F32)
    y = y + _dot(yc_ref[0], wo_ref[b_end:D_MODEL, :])
    out = _swiglu_residual(x_ref[0] + y, g_ref, wg_ref, wu_ref, wd_ref)
    if final_norm:
        out = _rms_norm(out, fin_ref[...])
    o_ref[0] = out


def _ffn_call(x2d, g, wg, wu, wd, layer):
    t = x2d.shape[0]
    const = lambda i: (0, 0)
    of_layer = lambda i: (layer, 0, 0)
    return pl.pallas_call(
        _ffn_kernel,
        grid=(t // FFN_TM,),
        in_specs=[
            pl.BlockSpec((FFN_TM, D_MODEL), lambda i: (i, 0)),
            pl.BlockSpec((1, D_MODEL), const),
            pl.BlockSpec((None, D_MODEL, D_FF), of_layer, pipeline_mode=pl.Buffered(1)),
            pl.BlockSpec((None, D_MODEL, D_FF), of_layer, pipeline_mode=pl.Buffered(1)),
            pl.BlockSpec((None, D_FF, D_MODEL), of_layer, pipeline_mode=pl.Buffered(1)),
        ],
        out_specs=pl.BlockSpec((FFN_TM, D_MODEL), lambda i: (i, 0)),
        out_shape=jax.ShapeDtypeStruct(x2d.shape, F32),
        compiler_params=pltpu.CompilerParams(
            dimension_semantics=("arbitrary",), vmem_limit_bytes=VMEM_LIMIT_BYTES),
        name="ffn",
    )(x2d, g, wg, wu, wd)


def _mix_ffn_call(x3d, ya, ybt, yc, wo, g, wg, wu, wd, fin, final_norm, layer):
    b, s, _ = x3d.shape
    tm = FFN_TM
    row = lambda bi, i: (bi, i, 0)
    const = lambda bi, i: (0, 0)
    of_layer = lambda bi, i: (layer, 0, 0)
    return pl.pallas_call(
        functools.partial(_mix_ffn_kernel, final_norm=final_norm),
        grid=(b, s // tm),
        in_specs=[
            pl.BlockSpec((1, tm, D_MODEL), row),
            pl.BlockSpec((1, tm, POOL_WIDTH), row),
            pl.BlockSpec((1, ATTN_WIDTH, tm), lambda bi, i: (bi, 0, i)),
            pl.BlockSpec((1, tm, CONV_CH), row),
            pl.BlockSpec((None, D_MODEL, D_MODEL), of_layer, pipeline_mode=pl.Buffered(1)),
            pl.BlockSpec((1, D_MODEL), const),
            pl.BlockSpec((None, D_MODEL, D_FF), of_layer, pipeline_mode=pl.Buffered(1)),
            pl.BlockSpec((None, D_MODEL, D_FF), of_layer, pipeline_mode=pl.Buffered(1)),
            pl.BlockSpec((None, D_FF, D_MODEL), of_layer, pipeline_mode=pl.Buffered(1)),
            pl.BlockSpec((1, D_MODEL), const),
        ],
        out_specs=pl.BlockSpec((1, tm, D_MODEL), row),
        out_shape=jax.ShapeDtypeStruct(x3d.shape, F32),
        compiler_params=pltpu.CompilerParams(
            dimension_semantics=("arbitrary", "arbitrary"), vmem_limit_bytes=VMEM_LIMIT_BYTES),
        name="mix_ffn",
    )(x3d, ya, ybt, yc, wo, g, wg, wu, wd, fin)


_C_POOL = 0
_C_GA = 256
_C_GG = 512
_C_K = 768
_C_END = _C_K + ATTN_HEADS * SLAB


def _shift_rows(buf_ref, rows, d):
    halo = buf_ref.shape[0] - rows
    return buf_ref[halo - d:halo - d + rows, :]


def _mixer_in_kernel(x_ref, g_ref, wm_ref, wqt_ref, wvt_ref, wzf_ref, fb_ref, pq_ref, pk_ref,
                     pw_ref, ps_ref, cw_ref, cb_ref, lg_ref, lb_ref,
                     ya_ref, qt_ref, k_ref, vt_ref, yc_ref,
                     pbuf, cbuf, sbuf, fcarry):
    i = pl.program_id(1)
    tm = MIX_TM

    @pl.when(i == 0)
    def _():
        pbuf[0:POOL_HALO, :] = jnp.zeros((POOL_HALO, POOL_WIDTH), F32)
        cbuf[0, 0:CONV_HALO, :] = jnp.zeros((CONV_HALO, CONV_CH), F32)
        sbuf[0:SCAN_HALO, :] = jnp.zeros((SCAN_HALO, LANES), F32)
        fcarry[...] = jnp.zeros((1, LANES), F32)

    hn = _rms_norm(x_ref[0], g_ref[...]).astype(BF16)

    z = _dot(hn, wzf_ref[...]) + fb_ref[...]
    ga = _dot(hn, wm_ref[:, _C_GA:_C_GG])
    gg = _dot(hn, wm_ref[:, _C_GG:_C_K])
    u = _dot(hn, wm_ref[:, _C_POOL:_C_GA])

    logf = jnp.minimum(z, 0.0) - jnp.log1p(jnp.exp(-jnp.abs(z)))
    sbuf[SCAN_HALO:SCAN_HALO + tm, :] = logf
    d = 1
    cur = logf
    while d < tm:
        cur = sbuf[SCAN_HALO:SCAN_HALO + tm, :] + _shift_rows(sbuf, tm, d)
        sbuf[SCAN_HALO:SCAN_HALO + tm, :] = cur
        d *= 2
    ftile = cur + fcarry[...]
    fcarry[...] = ftile[tm - 1:tm, :]

    f2 = ftile * LOG2E
    hi = f2.astype(BF16).astype(F32)
    r1 = f2 - hi
    mid = r1.astype(BF16).astype(F32)
    lo = (r1 - mid).astype(BF16).astype(F32)
    lane = lax.broadcasted_iota(jnp.int32, (1, LANES), 1)
    fs = jnp.where(lane < ATTN_HEADS, hi,
                   jnp.where(lane < 2 * ATTN_HEADS, mid,
                             jnp.where(lane < FS_ONE, lo,
                                       jnp.where(lane == FS_ONE, 1.0, 0.0)))).astype(BF16)

    rows = POOL_HALO + tm
    pbuf[POOL_HALO:rows, :] = u
    lane_p = lax.broadcasted_iota(jnp.int32, (1, POOL_WIDTH), 1)
    group = lane_p // POOL_GROUP_DIM
    ext = pbuf[...]
    sel = jnp.zeros((tm, POOL_WIDTH), F32)
    span = 1
    for gi, w in enumerate(POOL_WINDOWS):
        while span < w:
            ext = ext + pltpu.roll(ext, span, 0)
            span *= 2
        sel = jnp.where(group == gi, ext[POOL_HALO:, :], sel)
    win = jnp.zeros((1, POOL_WIDTH), F32)
    for gi, w in enumerate(POOL_WINDOWS):
        win = jnp.where(group == gi, float(w), win)
    pos = (i * tm + lax.broadcasted_iota(jnp.int32, (tm, 1), 0) + 1).astype(F32)
    count = jnp.minimum(pos, win)
    pooled = (sel / count - u).astype(BF16)
    pbuf[0:POOL_HALO, :] = pbuf[tm:tm + POOL_HALO, :]

    crow = CONV_HALO + tm
    cbuf[0, CONV_HALO:crow, :] = ga * jax.nn.sigmoid(gg)
    base = cbuf[0]
    for r in range(1, SUBLANES):
        cbuf[r] = pltpu.roll(base, r, 0)
    cb = cb_ref[...]
    lg = lg_ref[...]
    lb = lb_ref[...]
    for rc in range(tm // CONV_ROWS):
        acc = jnp.broadcast_to(cb, (CONV_ROWS, CONV_CH))
        for j in range(CONV_KERNEL):
            delay = CONV_KERNEL - 1 - j
            r, a = delay % SUBLANES, delay // SUBLANES
            lo_row = CONV_HALO + rc * CONV_ROWS - a * SUBLANES
            acc = acc + cw_ref[j:j + 1, :] * cbuf[r, lo_row:lo_row + CONV_ROWS, :]
        mu = jnp.mean(acc, axis=-1, keepdims=True)
        xc = acc - mu
        var = jnp.mean(xc * xc, axis=-1, keepdims=True)
        yn = xc * lax.rsqrt(var + NORM_EPS) * lg + lb
        yc_ref[0, rc * CONV_ROWS:(rc + 1) * CONV_ROWS, :] = _silu(yn).astype(BF16)
    cbuf[0, 0:CONV_HALO, :] = cbuf[0, tm:tm + CONV_HALO, :]

    qt = lax.dot_general(wqt_ref[...], hn, NT_DIMS, preferred_element_type=F32) * (LOG2E / math.sqrt(HEAD_DIM))
    ks = _dot(hn, wm_ref[:, _C_K:_C_END])
    vt = lax.dot_general(wvt_ref[...], hn, NT_DIMS, preferred_element_type=F32).astype(BF16)
    for c in range(tm // ATT_KB):
        vt_ref[0, c] = vt[:, c * ATT_KB:(c + 1) * ATT_KB]
    qt = qt + lax.dot_general(pq_ref[...], fs, NT_DIMS, preferred_element_type=F32)
    qt_ref[0, 0] = qt.astype(BF16)
    k_ref[0] = (ks + _dot(fs, pk_ref[...])).astype(BF16)
    ya_ref[0] = (_dot(pooled, pw_ref[...]) * ps_ref[...]).astype(BF16)


def _mixer_in_call(x3d, g, wm, wqt, wvt, wzf, fb, pq, pk, pw, ps, cw, cb, lg, lb):
    b, s, _ = x3d.shape
    tm = MIX_TM
    assert tm == ATT_TQ == ATT_KB
    const = lambda bi, i: (0, 0)
    row = lambda bi, i: (bi, i, 0)
    slabs = ATTN_HEADS * SLAB
    out_shape = (
        jax.ShapeDtypeStruct((b, s, POOL_WIDTH), BF16),
        jax.ShapeDtypeStruct((b, s // ATT_TQ, slabs, ATT_TQ), BF16),
        jax.ShapeDtypeStruct((b, s, slabs), BF16),
        jax.ShapeDtypeStruct((b, s // ATT_KB, ATTN_WIDTH, ATT_KB), BF16),
        jax.ShapeDtypeStruct((b, s, CONV_CH), BF16),
    )
    return pl.pallas_call(
        _mixer_in_kernel,
        grid=(b, s // tm),
        in_specs=[
            pl.BlockSpec((1, tm, D_MODEL), row),
            pl.BlockSpec((1, D_MODEL), const),
            pl.BlockSpec((D_MODEL, _C_END), const, pipeline_mode=pl.Buffered(1)),
            pl.BlockSpec((slabs, D_MODEL), const, pipeline_mode=pl.Buffered(1)),
            pl.BlockSpec((ATTN_WIDTH, D_MODEL), const, pipeline_mode=pl.Buffered(1)),
            pl.BlockSpec((D_MODEL, LANES), const),
            pl.BlockSpec((1, LANES), const),
            pl.BlockSpec((slabs, LANES), const),
            pl.BlockSpec((LANES, slabs), const),
            pl.BlockSpec((POOL_WIDTH, POOL_WIDTH), const),
            pl.BlockSpec((1, POOL_WIDTH), const),
            pl.BlockSpec((CONV_HALO, CONV_CH), const),
            pl.BlockSpec((1, CONV_CH), const),
            pl.BlockSpec((1, CONV_CH), const),
            pl.BlockSpec((1, CONV_CH), const),
        ],
        out_specs=(
            pl.BlockSpec((1, tm, POOL_WIDTH), row),
            pl.BlockSpec((1, 1, slabs, ATT_TQ), lambda bi, i: (bi, i, 0, 0)),
            pl.BlockSpec((1, tm, slabs), row),
            pl.BlockSpec((1, tm // ATT_KB, ATTN_WIDTH, ATT_KB), lambda bi, i: (bi, i, 0, 0)),
            pl.BlockSpec((1, tm, CONV_CH), row),
        ),
        out_shape=out_shape,
        scratch_shapes=[
            pltpu.VMEM((POOL_HALO + tm, POOL_WIDTH), F32),
            pltpu.VMEM((SUBLANES, CONV_HALO + tm, CONV_CH), F32),
            pltpu.VMEM((SCAN_HALO + tm, LANES), F32),
            pltpu.VMEM((1, LANES), F32),
        ],
        compiler_params=pltpu.CompilerParams(
            dimension_semantics=("arbitrary", "arbitrary"), vmem_limit_bytes=VMEM_LIMIT_BYTES),
        name="mixer_in",
    )(x3d, g, wm, wqt, wvt, wzf, fb, pq, pk, pw, ps, cw, cb, lg, lb)


_STREAMS = tuple((hh, qs) for hh in range(ATT_HG) for qs in range(ATT_TQ // ATT_QS))


def _attn_kernel(qt_ref, k_ref, vt_ref, o_ref, *scratch):
    i = pl.program_id(2)
    kb, ck, qs_w = ATT_KB, ATT_CK, ATT_QS
    n_chunks = kb // ck
    s_bufs = scratch[0:RING]
    p_bufs = scratch[RING:2 * RING]
    acc_buf = scratch[2 * RING]

    def kv_block(q_blk, n):
        return jnp.where(n == 0, q_blk, n - 1)

    def stage_a(n, par, sts=None):
        ahead = n > i
        q_blk = jnp.where(ahead, jnp.minimum(i + 1, pl.num_programs(2) - 1), i)
        row0 = pl.multiple_of(kv_block(q_blk, jnp.where(ahead, n - (i + 1), n)) * kb, kb)
        for st in (range(len(_STREAMS)) if sts is None else sts):
            hh, qs = _STREAMS[st]
            k_blk = k_ref[0, pl.ds(row0, kb), hh * SLAB:(hh + 1) * SLAB]
            qt = qt_ref[0, q_blk, hh * SLAB:(hh + 1) * SLAB, qs * qs_w:(qs + 1) * qs_w]
            s_bufs[par][st] = _dot(k_blk, qt)

    def stage_b(par, ms, masked, sts=None):
        new_ms, alphas = [], []
        for st in (range(len(_STREAMS)) if sts is None else sts):
            hh, qs = _STREAMS[st]

            def chunk(c, st=st, qs=qs):
                sc = s_bufs[par][st, c * ck:(c + 1) * ck, :]
                if masked and (c + 1) * ck - 1 > qs * qs_w:
                    kv_pos = c * ck + lax.broadcasted_iota(jnp.int32, (ck, qs_w), 0)
                    q_pos = qs * qs_w + lax.broadcasted_iota(jnp.int32, (ck, qs_w), 1)
                    sc = jnp.where(kv_pos <= q_pos, sc, NEG_BIG)
                return sc

            bmax = jnp.max(chunk(0), axis=0, keepdims=True)
            for c in range(1, n_chunks):
                bmax = jnp.maximum(bmax, jnp.max(chunk(c), axis=0, keepdims=True))
            m_new = jnp.maximum(ms[st], bmax)
            alphas.append(jnp.exp2(ms[st] - m_new))
            new_ms.append(m_new)
            for c in range(n_chunks):
                p_bufs[par][st, c * ck:(c + 1) * ck, :] = jnp.exp2(chunk(c) - m_new).astype(BF16)
        return tuple(new_ms), tuple(alphas)

    ones_rows = jnp.ones((ATT_VROWS - HEAD_DIM, kb), BF16)

    def stage_c(n, par, alphas, sts=None):
        jb = kv_block(i, n)
        for st in (range(len(_STREAMS)) if sts is None else sts):
            hh, qs = _STREAMS[st]
            v_blk = vt_ref[0, jb, hh * HEAD_DIM:(hh + 1) * HEAD_DIM, :]
            v_ext = jnp.concatenate([v_blk, ones_rows], axis=0)
            acc_buf[st] = alphas[st] * acc_buf[st] + _dot(v_ext, p_bufs[par][st])

    def tick(n, slot, ms, al1, al2, with_c=True):
        new_ms, new_al = [], []
        for st in range(len(_STREAMS)):
            stage_a(n + 2, (slot + 2) % RING, (st,))
            if with_c:
                stage_c(n - 2, (slot + 1) % RING, al2, (st,))
            m_st, a_st = stage_b(slot, ms, False, (st,))
            new_ms += m_st
            new_al += a_st
        return tuple(new_ms), tuple(new_al), al1

    def run_q_block(off):
        slot_of = lambda n: (n + off) % RING
        for st in range(len(_STREAMS)):
            acc_buf[st] = jnp.zeros((ATT_VROWS, qs_w), F32)
        ms = tuple(jnp.full((1, qs_w), NEG_BIG, F32) for _ in _STREAMS)

        if off == 0:
            @pl.when(i == 0)
            def _():
                stage_a(0, slot_of(0))
                stage_a(1, slot_of(1))

        ms, al1 = stage_b(slot_of(0), ms, True)
        stage_a(2, slot_of(2))
        state = (ms, al1, al1)
        state = lax.cond(i >= 1, lambda _: tick(1, slot_of(1), *state, with_c=False), lambda _: state, 0)

        def tick_ring(k, carry):
            n = RING * k + 2
            for r in range(RING):
                carry = tick(n + r, slot_of(2 + r), *carry)
            return carry

        n_ring = jnp.maximum(i - 1, 0)
        state = lax.fori_loop(0, n_ring // RING, tick_ring, state)
        left = lax.rem(n_ring, RING)

        state = lax.cond(left >= 1, lambda _: tick(i - left + 1, slot_of(2), *state), lambda _: state, 0)
        state = lax.cond(left == 2, lambda _: tick(i, slot_of(0), *state), lambda _: state, 0)
        _, al1, al2 = state
        for slot in range(RING):
            pl.when(jnp.logical_and(i >= 1, lax.rem(i + off + RING - 1, RING) == slot))(
                functools.partial(stage_c, i - 1, slot, al2))
        for slot in range(RING):
            pl.when(lax.rem(i + off, RING) == slot)(functools.partial(stage_c, i, slot, al1))

    off_is_one = lax.rem(i, RING) == 1
    pl.when(off_is_one)(functools.partial(run_q_block, 1))
    pl.when(jnp.logical_not(off_is_one))(functools.partial(run_q_block, 0))
    for st, (hh, qs) in enumerate(_STREAMS):
        acc = acc_buf[st]
        out = acc[0:HEAD_DIM, :] / acc[HEAD_DIM:HEAD_DIM + 1, :]
        o_ref[0, hh * HEAD_DIM:(hh + 1) * HEAD_DIM, qs * qs_w:(qs + 1) * qs_w] = out.astype(BF16)


def _attn_call(qt, ks, vt):
    b, s, _ = ks.shape
    tq = ATT_TQ
    n_groups = ATTN_HEADS // ATT_HG
    n_streams = len(_STREAMS)
    return pl.pallas_call(
        _attn_kernel,
        grid=(b, n_groups, s // tq),
        in_specs=[
            pl.BlockSpec((1, s // tq, ATT_HG * SLAB, tq), lambda bi, hg, i: (bi, 0, hg, 0)),
            pl.BlockSpec((1, s, ATT_HG * SLAB), lambda bi, hg, i: (bi, 0, hg)),
            pl.BlockSpec((1, s // ATT_KB, ATT_HG * HEAD_DIM, ATT_KB), lambda bi, hg, i: (bi, 0, hg, 0)),
        ],
        out_specs=pl.BlockSpec((1, ATT_HG * HEAD_DIM, tq), lambda bi, hg, i: (bi, hg, i)),
        out_shape=jax.ShapeDtypeStruct((b, ATTN_WIDTH, s), BF16),
        scratch_shapes=(
            [pltpu.VMEM((n_streams, ATT_KB, ATT_QS), F32)] * RING
            + [pltpu.VMEM((n_streams, ATT_KB, ATT_QS), BF16)] * RING
            + [pltpu.VMEM((n_streams, ATT_VROWS, ATT_QS), F32)]),
        compiler_params=pltpu.CompilerParams(
            dimension_semantics=("arbitrary", "arbitrary", "arbitrary"), vmem_limit_bytes=VMEM_LIMIT_BYTES),
        name="attn",
    )(qt, ks, vt)


def _block_diag(pool_w):
    g, c, d = pool_w.shape
    out = jnp.zeros((g * c, g * d), pool_w.dtype)
    for gi in range(g):
        out = lax.dynamic_update_slice(out, pool_w[gi], (gi * c, gi * d))
    return out


def _head_slabs(w):
    dm = w.shape[0]
    w3 = w.reshape(dm, ATTN_HEADS, HEAD_DIM)
    return jnp.pad(w3, ((0, 0), (0, 0), (0, SLAB - HEAD_DIM))).reshape(dm, ATTN_HEADS * SLAB)


def _placement():
    pq = np.zeros((LANES, ATTN_HEADS * SLAB), np.float32)
    pk = np.zeros((LANES, ATTN_HEADS * SLAB), np.float32)
    for h in range(ATTN_HEADS):
        base = h * SLAB + HEAD_DIM
        for part in range(3):
            pq[part * ATTN_HEADS + h, base + part] = 1.0
            pq[FS_ONE, base + 3 + part] = 1.0
            pk[FS_ONE, base + part] = 1.0
            pk[part * ATTN_HEADS + h, base + 3 + part] = -1.0
    return jnp.asarray(pq.T, BF16), jnp.asarray(pk, BF16)


def _rep3(a):
    a3 = jnp.concatenate([a, a, a], axis=-1)
    return jnp.pad(a3, ((0, 0), (0, LANES - a3.shape[-1])))


def kernel(x, ffn1_norm, ffn1_w_gate, ffn1_w_up, ffn1_w_down, mix_norm, w_in, pool_w, pool_scale, forget_bias, conv_w, conv_b, conv_ln_g, conv_ln_b, w_out, ffn2_norm, ffn2_w_gate, ffn2_w_up, ffn2_w_down, final_norm):
    b, s, d = x.shape
    depth = w_in.shape[0]
    t = b * s
    x2d = x.reshape(t, d)
    fin = final_norm.reshape(1, d)
    c_q = POOL_WIDTH
    c_k = c_q + ATTN_WIDTH
    c_v = c_k + ATTN_WIDTH
    c_zf = c_v + ATTN_WIDTH
    c_glu = c_zf + ATTN_HEADS
    pq_t, pk = _placement()
    ffn1 = tuple(w.astype(BF16) for w in (ffn1_w_gate, ffn1_w_up, ffn1_w_down))
    ffn2 = tuple(w.astype(BF16) for w in (ffn2_w_gate, ffn2_w_up, ffn2_w_down))
    wo = w_out.astype(BF16)
    for l in range(depth):
        x2d = _ffn_call(x2d, ffn1_norm[l].reshape(1, d), *ffn1, l)
        wl = w_in[l]
        wm = jnp.concatenate([wl[:, :c_q], wl[:, c_glu:], _head_slabs(wl[:, c_k:c_v])], axis=1).astype(BF16)
        wqt = _head_slabs(wl[:, c_q:c_k]).T.astype(BF16)
        wvt = wl[:, c_v:c_zf].T.astype(BF16)
        wzf = _rep3(wl[:, c_zf:c_glu]).astype(BF16)
        fb = _rep3(forget_bias[l].reshape(1, ATTN_HEADS))
        pw = _block_diag(pool_w[l]).astype(BF16)
        cw = jnp.pad(conv_w[l], ((0, CONV_HALO - CONV_KERNEL), (0, 0)))
        ya, qt, ks, vt, yc = _mixer_in_call(
            x2d.reshape(b, s, d), mix_norm[l].reshape(1, d), wm, wqt, wvt, wzf, fb, pq_t, pk, pw,
            pool_scale[l].reshape(1, POOL_WIDTH), cw, conv_b[l].reshape(1, CONV_CH),
            conv_ln_g[l].reshape(1, CONV_CH), conv_ln_b[l].reshape(1, CONV_CH))
        ybt = _attn_call(qt, ks, vt)
        x2d = _mix_ffn_call(x2d.reshape(b, s, d), ya, ybt, yc, wo, ffn2_norm[l].reshape(1, d), *ffn2,
                            fin, l == depth - 1, l).reshape(t, d)
    return x2d.reshape(b, s, d)
```
